```python
import jax, jax.numpy as jnp
from jax import lax
import numpy as np

D_MODEL = 4096
BATCH = 4
SEQ = 2048
DEPTH = 2
DEC_BATCH = 8
DEC_SEQ = 1
PAST_LEN = 16384
PAGE_SIZE = 128

N_META = 16
HM = 8
DV_M = D_MODEL // (2 * HM)
DQK_M = DV_M // 2
HF = D_MODEL // 256
DH_F = (D_MODEL // 2) // HF
D_FF = -(-8 * D_MODEL // (3 * 256)) * 256
CHUNK = 128
Q_BLOCK = 128
EPS = 1e-6
SCALE_F = DH_F ** -0.5
FOX_F_BIAS_LO = 2.0
FOX_F_BIAS_HI = 10.0

SPLITS = (HM * DQK_M, HM * DQK_M, HM * DV_M, HM * DV_M, HM, HM,
          HF * DH_F, HF * DH_F, HF * DH_F, HF, D_MODEL, D_MODEL)
N_IN = sum(SPLITS)
SPLIT_POINTS = tuple(int(s) for s in np.cumsum(SPLITS)[:-1])

F32 = jnp.float32

kernel_name = 'hybrid_mlstm_fox_decoder_step'


def rms_norm(x, g):
    x32 = x.astype(F32)
    y = x32 * lax.rsqrt(jnp.mean(x32 * x32, axis=-1, keepdims=True) + EPS)
    return (y * g.astype(F32)).astype(x.dtype)


def to_blocks(a, size):
    b, t = a.shape[0], a.shape[1]
    return jnp.moveaxis(a.reshape((b, t // size, size) + a.shape[2:]), 1, 0)


def from_blocks(a):
    a = jnp.moveaxis(a, 0, 1)
    return a.reshape((a.shape[0], a.shape[1] * a.shape[2]) + a.shape[3:])


def in_projection(h, w_in, b_in, qn_g, kn_g):
    b, t, _ = h.shape
    z = h @ w_in + b_in
    mq, mk, mv, mo, mi, mf, fq, fk, fv, ff, ga, gb = jnp.split(z, SPLIT_POINTS, axis=-1)
    mlstm_in = (mq.reshape(b, t, HM, DQK_M),
                mk.reshape(b, t, HM, DQK_M) * (DQK_M ** -0.5),
                mv.reshape(b, t, HM, DV_M),
                mi.astype(F32),
                jax.nn.log_sigmoid(mf.astype(F32)))
    fox_in = (rms_norm(fq.reshape(b, t, HF, DH_F), qn_g),
              rms_norm(fk.reshape(b, t, HF, DH_F), kn_g),
              fv.reshape(b, t, HF, DH_F),
              jax.nn.log_sigmoid(ff.astype(F32)))
    return mlstm_in, fox_in, (mo, ga, gb)


def mlstm_chunk(carry, xs):
    c0, n0, m0 = (a.astype(F32) for a in carry)
    q, k, v, logi, logf = xs
    q, k, v = q.astype(F32), k.astype(F32), v.astype(F32)
    L = q.shape[1]
    bcum = jnp.cumsum(logf, axis=1)
    g = bcum + m0[:, None, :]
    d = bcum[:, :, None, :] - bcum[:, None, :, :] + logi[:, None, :, :]
    causal = jnp.tril(jnp.ones((L, L), dtype=bool))
    d = jnp.where(causal[None, :, :, None], d, -jnp.inf)
    m = jnp.maximum(g, jnp.max(d, axis=2))
    w = jnp.exp(d - m[:, :, None, :])
    inter = jnp.exp(g - m)
    s = jnp.einsum('bthd,bshd->btsh', q, k) * w
    num = jnp.einsum('btsh,bshv->bthv', s, v) + inter[..., None] * jnp.einsum('bthd,bhdv->bthv', q, c0)
    den = jnp.sum(s, axis=2) + inter * jnp.einsum('bthd,bhd->bth', q, n0)
    h = num / jnp.maximum(jnp.abs(den), jnp.exp(-m))[..., None]
    m_end = m[:, -1]
    w_end = w[:, -1]
    decay = jnp.exp(g[:, -1] - m_end)
    c1 = decay[..., None, None] * c0 + jnp.einsum('bsh,bshd,bshv->bhdv', w_end, k, v)
    n1 = decay[..., None] * n0 + jnp.einsum('bsh,bshd->bhd', w_end, k)
    return (c1, n1, m_end), h


def mlstm_prompt(q, k, v, logi, logf):
    b = q.shape[0]
    state = (jnp.zeros((b, HM, DQK_M, DV_M), F32), jnp.zeros((b, HM, DQK_M), F32), jnp.zeros((b, HM), F32))
    arrs = (q, k, v, logi, logf)
    state, h_meta = mlstm_chunk(state, tuple(a[:, :N_META] for a in arrs))
    state, h_real = lax.scan(mlstm_chunk, state, tuple(to_blocks(a[:, N_META:], CHUNK) for a in arrs))
    return jnp.concatenate([h_meta, from_blocks(h_real)], axis=1), state


def fox_block(q, fq, qpos, k, v, fk, kpos):
    s = jnp.einsum('bqhd,bkhd->bhqk', q, k).astype(F32) * SCALE_F
    s = s + jnp.transpose(fq, (0, 2, 1))[..., None] - jnp.transpose(fk, (0, 2, 1))[:, :, None, :]
    mask = kpos[None, :] <= qpos[:, None]
    s = jnp.where(mask[None, None], s, -jnp.inf)
    p = jax.nn.softmax(s, axis=-1)
    return jnp.einsum('bhqk,bkhd->bqhd', p.astype(v.dtype), v)


def fox_prompt(q, k, v, logf):
    t = q.shape[1]
    fcum = jnp.cumsum(logf, axis=1)
    pos = jnp.arange(t)
    o_meta = fox_block(q[:, :N_META], fcum[:, :N_META], pos[:N_META],
                       k[:, :N_META], v[:, :N_META], fcum[:, :N_META], pos[:N_META])
    xs = (to_blocks(q[:, N_META:], Q_BLOCK), to_blocks(fcum[:, N_META:], Q_BLOCK),
          pos[N_META:].reshape(-1, Q_BLOCK))
    o_real = lax.map(lambda a: fox_block(a[0], a[1], a[2], k, v, fcum, pos), xs)
    return jnp.concatenate([o_meta, from_blocks(o_real)], axis=1)


def fox_decode(q, k, v, logf, k_past, v_past, lf_past):
    s_len = q.shape[1]
    p_len = k_past.shape[1]
    fn = jnp.cumsum(logf, axis=1)
    rev = lax.cumsum(lf_past.astype(F32), axis=1, reverse=True)
    r = jnp.concatenate([rev[:, 1:], jnp.zeros_like(rev[:, :1])], axis=1)
    fq = jnp.transpose(fn, (0, 2, 1))[..., None]
    s_past = (jnp.einsum('bqhd,bkhd->bhqk', q, k_past).astype(F32) * SCALE_F
              + fq + jnp.transpose(r, (0, 2, 1))[:, :, None, :])
    s_new = (jnp.einsum('bqhd,bkhd->bhqk', q, k).astype(F32) * SCALE_F
             + fq - jnp.transpose(fn, (0, 2, 1))[:, :, None, :])
    causal = jnp.tril(jnp.ones((s_len, s_len), dtype=bool))
    s_new = jnp.where(causal[None, None], s_new, -jnp.inf)
    p = jax.nn.softmax(jnp.concatenate([s_past, s_new], axis=-1), axis=-1)
    return (jnp.einsum('bhqk,bkhd->bqhd', p[..., :p_len].astype(v_past.dtype), v_past)
            + jnp.einsum('bhqk,bkhd->bqhd', p[..., p_len:].astype(v.dtype), v))


def merge(x, h_m, h_f, mo, ga, gb, mh_g, p_a, p_b, w_out):
    b, t, _ = x.shape
    h_m = rms_norm(h_m.astype(x.dtype), mh_g.reshape(HM, DV_M)).reshape(b, t, HM * DV_M) * jax.nn.sigmoid(mo)
    u = jax.nn.sigmoid(ga) * (h_m @ p_a) + jax.nn.sigmoid(gb) * (h_f.reshape(b, t, HF * DH_F) @ p_b)
    return x + u @ w_out


def ffn(x, g, w_gate, w_up, w_down):
    h = rms_norm(x, g)
    return x + (jax.nn.silu(h @ w_gate) * (h @ w_up)) @ w_down


def setup_inputs(seed: int = 0) -> dict:
    key = jax.random.key(seed)
    ks = jax.random.split(key, 24)
    n_pages = PAST_LEN // PAGE_SIZE
    n_used = DEC_BATCH * n_pages
    n_pool = n_used + max(1, n_used // 4)
    nrm = lambda k, shape: jax.random.normal(k, shape, F32)
    page_table = jax.random.permutation(ks[0], n_pool)[:n_used].reshape(DEC_BATCH, n_pages).astype(jnp.int32)
    fox_bias = jnp.linspace(FOX_F_BIAS_LO, FOX_F_BIAS_HI, HF, dtype=F32)
    offsets = jnp.concatenate([
        jnp.zeros((sum(SPLITS[:5]),), F32),
        jnp.linspace(3.0, 6.0, HM, dtype=F32),
        jnp.zeros((3 * HF * DH_F,), F32),
        fox_bias,
        jnp.zeros((2 * D_MODEL,), F32)])
    return {
        'x_prompt': nrm(ks[1], (BATCH, SEQ, D_MODEL)),
        'x_sample': nrm(ks[2], (DEC_BATCH, DEC_SEQ, D_MODEL)),
        'cache_k': nrm(ks[3], (DEPTH, n_pool, PAGE_SIZE, HF, DH_F)),
        'cache_v': nrm(ks[4], (DEPTH, n_pool, PAGE_SIZE, HF, DH_F)),
        'cache_lf': jax.nn.log_sigmoid(fox_bias + 0.5 * nrm(ks[5], (DEPTH, n_pool, PAGE_SIZE, HF))),
        'state_c': nrm(ks[6], (DEPTH, DEC_BATCH, HM, DQK_M, DV_M)),
        'state_n': nrm(ks[7], (DEPTH, DEC_BATCH, HM, DQK_M)),
        'state_m': 0.5 * nrm(ks[8], (DEPTH, DEC_BATCH, HM)),
        'page_table': page_table,
        'meta': nrm(ks[9], (N_META, D_MODEL)),
        'norm1_g': 1.0 + 0.02 * nrm(ks[10], (DEPTH, D_MODEL)),
        'w_in': nrm(ks[11], (DEPTH, D_MODEL, N_IN)) * D_MODEL ** -0.5,
        'b_in': 0.02 * nrm(ks[12], (DEPTH, N_IN)) + offsets[None],
        'q_norm_g': 1.0 + 0.02 * nrm(ks[13], (DEPTH, DH_F)),
        'k_norm_g': 1.0 + 0.02 * nrm(ks[14], (DEPTH, DH_F)),
        'mh_norm_g': 1.0 + 0.02 * nrm(ks[15], (DEPTH, HM * DV_M)),
        'p_a': nrm(ks[16], (DEPTH, HM * DV_M, D_MODEL)) * (HM * DV_M) ** -0.5,
        'p_b': nrm(ks[17], (DEPTH, HF * DH_F, D_MODEL)) * (HF * DH_F) ** -0.5,
        'w_out': nrm(ks[18], (DEPTH, D_MODEL, D_MODEL)) * D_MODEL ** -0.5,
        'norm2_g': 1.0 + 0.02 * nrm(ks[19], (DEPTH, D_MODEL)),
        'w_gate': nrm(ks[20], (DEPTH, D_MODEL, D_FF)) * D_MODEL ** -0.5,
        'w_up': nrm(ks[21], (DEPTH, D_MODEL, D_FF)) * D_MODEL ** -0.5,
        'w_down': nrm(ks[22], (DEPTH, D_FF, D_MODEL)) * D_FF ** -0.5,
    }


def reference(x_prompt, x_sample, cache_k, cache_v, cache_lf, state_c, state_n, state_m, page_table,
              meta, norm1_g, w_in, b_in, q_norm_g, k_norm_g, mh_norm_g, p_a, p_b, w_out,
              norm2_g, w_gate, w_up, w_down):
    b = x_prompt.shape[0]
    db = x_sample.shape[0]
    xp = jnp.concatenate([jnp.broadcast_to(meta[None].astype(x_prompt.dtype), (b, N_META, D_MODEL)), x_prompt], axis=1)
    xs = x_sample
    kp_l, vp_l, lfp_l, cp_l, np_l, mp_l = [], [], [], [], [], []
    ks_l, vs_l, lfs_l, cs_l, ns_l, ms_l = [], [], [], [], [], []
    for l in range(DEPTH):
        m_in, f_in, g_in = in_projection(rms_norm(xp, norm1_g[l]), w_in[l], b_in[l], q_norm_g[l], k_norm_g[l])
        h_m, (c_p, n_p, m_p) = mlstm_prompt(*m_in)
        h_f = fox_prompt(*f_in)
        xp = merge(xp, h_m, h_f, *g_in, mh_norm_g[l], p_a[l], p_b[l], w_out[l])
        xp = ffn(xp, norm2_g[l], w_gate[l], w_up[l], w_down[l])
        kp_l.append(f_in[1]); vp_l.append(f_in[2]); lfp_l.append(f_in[3])
        cp_l.append(c_p); np_l.append(n_p); mp_l.append(m_p)
        m_in, f_in, g_in = in_projection(rms_norm(xs, norm1_g[l]), w_in[l], b_in[l], q_norm_g[l], k_norm_g[l])
        (c_s, n_s, m_s), h_m = mlstm_chunk((state_c[l], state_n[l], state_m[l]), m_in)
        k_past = cache_k[l][page_table].reshape(db, -1, HF, DH_F)
        v_past = cache_v[l][page_table].reshape(db, -1, HF, DH_F)
        lf_past = cache_lf[l][page_table].reshape(db, -1, HF)
        h_f = fox_decode(*f_in, k_past, v_past, lf_past)
        xs = merge(xs, h_m, h_f, *g_in, mh_norm_g[l], p_a[l], p_b[l], w_out[l])
        xs = ffn(xs, norm2_g[l], w_gate[l], w_up[l], w_down[l])
        ks_l.append(f_in[1]); vs_l.append(f_in[2]); lfs_l.append(f_in[3])
        cs_l.append(c_s); ns_l.append(n_s); ms_l.append(m_s)
    y_prompt = xp[:, N_META:]
    y_sample = xs
    k_prompt, v_prompt, lf_prompt = jnp.stack(kp_l), jnp.stack(vp_l), jnp.stack(lfp_l)
    c_prompt, n_prompt, m_prompt = jnp.stack(cp_l), jnp.stack(np_l), jnp.stack(mp_l)
    k_sample, v_sample, lf_sample = jnp.stack(ks_l), jnp.stack(vs_l), jnp.stack(lfs_l)
    c_sample, n_sample, m_sample = jnp.stack(cs_l), jnp.stack(ns_l), jnp.stack(ms_l)
    return (y_prompt, y_sample, k_prompt, v_prompt, lf_prompt, c_prompt, n_prompt, m_prompt,
            k_sample, v_sample, lf_sample, c_sample, n_sample, m_sample)
```

```python
import functools

import jax
import jax.numpy as jnp
from jax import lax
from jax.experimental import pallas as pl
from jax.experimental.pallas import tpu as pltpu

F32 = jnp.float32
BF16 = jnp.bfloat16
EPS = 1e-6
LANES = 128
CHUNK = 128
SMALL_ROWS = 128
VMEM_LIMIT = 52 * 1024 * 1024
NT_DIMS = (((1,), (1,)), ((), ()))


def _cparams(sem):
    return pltpu.CompilerParams(dimension_semantics=sem, vmem_limit_bytes=VMEM_LIMIT)


def _norm_kernel(x_ref, g_ref, o_ref):
    x = x_ref[...]
    y = x * lax.rsqrt(jnp.mean(x * x, axis=-1, keepdims=True) + EPS)
    o_ref[...] = (y * g_ref[...]).astype(o_ref.dtype)


def rmsnorm_rows(x, g):
    m, d = x.shape
    tm = min(256, m)
    return pl.pallas_call(
        _norm_kernel, grid=(m // tm,),
        in_specs=[pl.BlockSpec((tm, d), lambda i: (i, 0)), pl.BlockSpec((1, d), lambda i: (0, 0))],
        out_specs=pl.BlockSpec((tm, d), lambda i: (i, 0)),
        out_shape=jax.ShapeDtypeStruct((m, d), BF16),
        compiler_params=_cparams(("parallel",)), name="rmsnorm")(x, g.reshape(1, d))


def _mm_kernel(*refs, n_lhs, ws, n_extra, n_const, epi):
    it = iter(refs)
    lhs_big = [next(it) for _ in range(n_lhs)]
    lhs_small = [next(it) for _ in range(n_lhs)]
    w_refs = [next(it) for _ in ws]
    b_refs = [next(it) if has_bias else None for (_, has_bias) in ws]
    ex_big = [next(it) for _ in range(n_extra)]
    ex_small = [next(it) for _ in range(n_extra)]
    consts = [next(it) for _ in range(n_const)]
    out_big = next(it)
    out_small = next(it)

    def run(lhs, extras, out):
        accs = []
        for k, (li, _) in enumerate(ws):
            a = jnp.dot(lhs[li][...], w_refs[k][...], preferred_element_type=F32)
            if b_refs[k] is not None:
                a = a + b_refs[k][...]
            accs.append(a)
        out[...] = epi(accs, [e[...] for e in extras], [c[...] for c in consts]).astype(out.dtype)

    run(lhs_big, ex_big, out_big)

    @pl.when(pl.program_id(1) == 0)
    def _():
        run(lhs_small, ex_small, out_small)


def fused_matmul(lhs, weights, extras, consts, epi, n_cols, out_dtype, tm, tn, name):
    r = lhs[0][0].shape[0]
    s = lhs[0][1].shape[0]
    tm = min(tm, r)
    assert r % tm == 0 and n_cols % tn == 0
    in_specs, args = [], []
    for big, _ in lhs:
        in_specs.append(pl.BlockSpec((tm, big.shape[1]), lambda j, i: (i, 0)))
        args.append(big)
    for _, small in lhs:
        in_specs.append(pl.BlockSpec(small.shape, lambda j, i: (0, 0)))
        args.append(small)
    for w, _, off, _ in weights:
        assert off % tn == 0
        in_specs.append(pl.BlockSpec((w.shape[0], tn), lambda j, i, o=off // tn: (0, j + o)))
        args.append(w)
    for _, _, off, b in weights:
        if b is not None:
            in_specs.append(pl.BlockSpec((1, tn), lambda j, i, o=off // tn: (0, j + o)))
            args.append(b)
    for big, _, off in extras:
        assert off % tn == 0
        in_specs.append(pl.BlockSpec((tm, tn), lambda j, i, o=off // tn: (i, j + o)))
        args.append(big)
    for _, small, off in extras:
        in_specs.append(pl.BlockSpec((s, tn), lambda j, i, o=off // tn: (0, j + o)))
        args.append(small)
    for c in consts:
        if c.shape == (1, n_cols):
            in_specs.append(pl.BlockSpec((1, tn), lambda j, i: (0, j)))
        else:
            in_specs.append(pl.BlockSpec(c.shape, lambda j, i, nd=c.ndim: (0,) * nd))
        args.append(c)
    kern = functools.partial(
        _mm_kernel, n_lhs=len(lhs), ws=[(li, b is not None) for _, li, _, b in weights],
        n_extra=len(extras), n_const=len(consts), epi=epi)
    return pl.pallas_call(
        kern, grid=(n_cols // tn, r // tm), in_specs=in_specs,
        out_specs=[pl.BlockSpec((tm, tn), lambda j, i: (i, j)), pl.BlockSpec((s, tn), lambda j, i: (0, j))],
        out_shape=[jax.ShapeDtypeStruct((r, n_cols), out_dtype), jax.ShapeDtypeStruct((s, n_cols), out_dtype)],
        compiler_params=_cparams(("parallel", "arbitrary")), name=name)(*args)


def _epi_scale(accs, extras, consts):
    return accs[0] * consts[0]


def _epi_identity(accs, extras, consts):
    return accs[0]


def _epi_headnorm(accs, extras, consts):
    y, g = accs[0], consts[0]
    outs = []
    for h in range(y.shape[1] // LANES):
        yh = y[:, h * LANES:(h + 1) * LANES]
        outs.append(yh * lax.rsqrt(jnp.mean(yh * yh, axis=-1, keepdims=True) + EPS) * g)
    return jnp.concatenate(outs, axis=-1)


def _epi_gated2(accs, extras, consts):
    return jax.nn.sigmoid(extras[0]) * accs[0] + jax.nn.sigmoid(extras[1]) * accs[1]


def _epi_residual(accs, extras, consts):
    return extras[0] + accs[0]


def _epi_swiglu(accs, extras, consts):
    return jax.nn.silu(accs[0]) * accs[1]


def _gates_kernel(h_ref, w_ref, b_ref, c0_ref, l_ref, cb_ref, cf_ref, lt_ref, cbt_ref, cft_ref, carry,
                  *, n_lin, n_gate, cs):
    @pl.when(pl.program_id(1) == 0)
    def _():
        carry[...] = c0_ref[...]

    z = jnp.dot(h_ref[...], w_ref[...], preferred_element_type=F32) + b_ref[...]
    lane = lax.broadcasted_iota(jnp.int32, z.shape, 1)
    logs = jnp.where(lane < n_lin, z, jnp.where(lane < n_gate, jax.nn.log_sigmoid(z), 0.0))
    tri = (lax.broadcasted_iota(jnp.int32, (cs, cs), 1) <= lax.broadcasted_iota(jnp.int32, (cs, cs), 0)).astype(F32)
    for s in range(z.shape[0] // cs):
        rows = slice(s * cs, (s + 1) * cs)
        lc = logs[rows]
        cb = jnp.dot(tri, lc, precision=lax.Precision.HIGHEST, preferred_element_type=F32)
        cf = cb + carry[...]
        carry[...] = cf[cs - 1:cs, :]
        l_ref[rows, :] = lc
        cb_ref[rows, :] = cb
        cf_ref[rows, :] = cf
        lt_ref[:, rows] = lc.T
        cbt_ref[:, rows] = cb.T
        cft_ref[:, rows] = cf.T


def gate_logs(h, w, col_off, bias, carry0, n_seq, n_lin, n_gate):
    r, d = h.shape
    seq = r // n_seq
    tile = min(512, seq)
    nt = seq // tile
    cs = min(CHUNK, tile)
    row = pl.BlockSpec((tile, LANES), lambda b, t: (b * nt + t, 0))
    col = pl.BlockSpec((LANES, tile), lambda b, t: (0, b * nt + t))
    kern = functools.partial(_gates_kernel, n_lin=n_lin, n_gate=n_gate, cs=cs)
    return pl.pallas_call(
        kern, grid=(n_seq, nt),
        in_specs=[pl.BlockSpec((tile, d), lambda b, t: (b * nt + t, 0)),
                  pl.BlockSpec((d, LANES), lambda b, t, o=col_off // LANES: (0, o)),
                  pl.BlockSpec((1, LANES), lambda b, t, o=col_off // LANES: (0, o)),
                  pl.BlockSpec((1, LANES), lambda b, t: (0, 0))],
        out_specs=[row, row, row, col, col, col],
        out_shape=[jax.ShapeDtypeStruct((r, LANES), F32)] * 3 + [jax.ShapeDtypeStruct((LANES, r), F32)] * 3,
        scratch_shapes=[pltpu.VMEM((1, LANES), F32)],
        compiler_params=_cparams(("arbitrary", "arbitrary")), name="gate_logs")(h, w, bias, carry0)


def _mlstm_head(q, k, v, bcol, brow, icol, irow, c0, n0, m0, nv):
    ln = q.shape[0]
    g = bcol + m0
    d = bcol - brow + irow
    ti = lax.broadcasted_iota(jnp.int32, (ln, ln), 0)
    si = lax.broadcasted_iota(jnp.int32, (ln, ln), 1)
    d = jnp.where(si <= ti, d, -jnp.inf)
    m = jnp.maximum(g, jnp.max(d, axis=1, keepdims=True))
    w = jnp.exp(d - m)
    inter = jnp.exp(g - m)
    s = lax.dot_general(q, k, NT_DIMS, preferred_element_type=F32) * w
    num = (jnp.dot(s.astype(BF16), v, preferred_element_type=F32)
           + inter * jnp.dot(q, c0.astype(BF16), preferred_element_type=F32))
    den = jnp.sum(s, axis=1, keepdims=True) + inter * jnp.sum(q.astype(F32) * n0, axis=1, keepdims=True)
    h = num / jnp.maximum(jnp.abs(den), jnp.exp(-m))
    m_end = m[nv - 1:nv]
    sidx = lax.broadcasted_iota(jnp.int32, (ln, 1), 0)
    w_end = jnp.where(sidx < nv, jnp.exp(bcol[nv - 1:nv] - bcol + icol - m_end), 0.0)
    decay = jnp.exp(g[nv - 1:nv] - m_end)
    kw = k.astype(F32) * w_end
    c1 = decay * c0 + jnp.dot(kw.T.astype(BF16), v, preferred_element_type=F32)
    n1 = decay * n0 + jnp.sum(kw, axis=0, keepdims=True)
    return h, c1, n1, m_end


def _mlstm_kernel(q_ref, k_ref, v_ref, og_ref, g_ref, l_ref, cb_ref, lt_ref, cbt_ref, ci_ref, ni_ref, mi_ref,
                  hm_ref, co_ref, no_ref, mo_ref, c_s, n_s, m_s, *, n_heads, dqk, dv, nv):
    c = pl.program_id(1)

    @pl.when(c == 0)
    def _():
        c_s[...] = ci_ref[...]
        n_s[...] = ni_ref[...]
        m_s[...] = mi_ref[...]

    logs, cb, logs_t, cb_t = l_ref[...], cb_ref[...], lt_ref[...], cbt_ref[...]
    for h in range(n_heads):
        hh, c1, n1, m_end = _mlstm_head(
            q_ref[:, h * dqk:(h + 1) * dqk], k_ref[:, h * dqk:(h + 1) * dqk], v_ref[:, h * dv:(h + 1) * dv],
            cb[:, n_heads + h:n_heads + h + 1], cb_t[n_heads + h:n_heads + h + 1, :],
            logs[:, h:h + 1], logs_t[h:h + 1, :], c_s[h], n_s[h], m_s[h][:, 0:1], nv)
        c_s[h] = c1
        n_s[h] = n1
        m_s[h] = jnp.broadcast_to(m_end, (1, LANES))
        cols = slice(h * dv, (h + 1) * dv)
        hn = hh * lax.rsqrt(jnp.mean(hh * hh, axis=-1, keepdims=True) + EPS) * g_ref[:, cols]
        hm_ref[:, cols] = (hn * jax.nn.sigmoid(og_ref[:, cols])).astype(hm_ref.dtype)

    @pl.when(c == pl.num_programs(1) - 1)
    def _():
        co_ref[0] = c_s[...]
        no_ref[0] = n_s[...]
        mo_ref[0] = m_s[...]


def mlstm_chunks(qkv, og, og_off, mh_g, logs, cb, logs_t, cb_t, c_init, n_init, m_init, n_seq, nv):
    r = qkv.shape[0]
    n_heads, dqk, dv = c_init.shape
    nc = r // n_seq // CHUNK
    hq, hv = n_heads * dqk, n_heads * dv
    assert hv == 2 * hq and og_off % hv == 0
    rows = lambda b, c: b * nc + c
    kern = functools.partial(_mlstm_kernel, n_heads=n_heads, dqk=dqk, dv=dv, nv=nv)
    full3 = lambda shape: pl.BlockSpec(shape, lambda b, c: (0, 0, 0))
    return pl.pallas_call(
        kern, grid=(n_seq, nc),
        in_specs=[pl.BlockSpec((CHUNK, hq), lambda b, c: (rows(b, c), 0)),
                  pl.BlockSpec((CHUNK, hq), lambda b, c: (rows(b, c), 1)),
                  pl.BlockSpec((CHUNK, hv), lambda b, c: (rows(b, c), 1)),
                  pl.BlockSpec((CHUNK, hv), lambda b, c, o=og_off // hv: (rows(b, c), o)),
                  pl.BlockSpec((1, hv), lambda b, c: (0, 0)),
                  pl.BlockSpec((CHUNK, LANES), lambda b, c: (rows(b, c), 0)),
                  pl.BlockSpec((CHUNK, LANES), lambda b, c: (rows(b, c), 0)),
                  pl.BlockSpec((LANES, CHUNK), lambda b, c: (0, rows(b, c))),
                  pl.BlockSpec((LANES, CHUNK), lambda b, c: (0, rows(b, c))),
                  full3(c_init.shape), full3(n_init.shape), full3(m_init.shape)],
        out_specs=[pl.BlockSpec((CHUNK, hv), lambda b, c: (rows(b, c), 0)),
                   pl.BlockSpec((1,) + c_init.shape, lambda b, c: (b, 0, 0, 0)),
                   pl.BlockSpec((1,) + n_init.shape, lambda b, c: (b, 0, 0, 0)),
                   pl.BlockSpec((1,) + m_init.shape, lambda b, c: (b, 0, 0, 0))],
        out_shape=[jax.ShapeDtypeStruct((r, hv), BF16),
                   jax.ShapeDtypeStruct((n_seq,) + c_init.shape, F32),
                   jax.ShapeDtypeStruct((n_seq,) + n_init.shape, F32),
                   jax.ShapeDtypeStruct((n_seq,) + m_init.shape, F32)],
        scratch_shapes=[pltpu.VMEM(c_init.shape, F32), pltpu.VMEM(n_init.shape, F32), pltpu.VMEM(m_init.shape, F32)],
        compiler_params=_cparams(("arbitrary", "arbitrary")), name="mlstm_chunks")(
            qkv, qkv, qkv, og, mh_g, logs, cb, logs_t, cb_t, c_init, n_init, m_init)


def _mlstm_step_kernel(q_ref, k_ref, v_ref, og_ref, g_ref, li_ref, lf_ref, c_ref, n_ref, m_ref,
                       hm_ref, co_ref, no_ref, mo_ref, *, n_heads):
    q, k = q_ref[0].astype(F32), k_ref[0].astype(F32)
    v = v_ref[0].astype(F32)
    pad = jnp.zeros((LANES - n_heads, q.shape[1]), F32)
    q_t = jnp.concatenate([q, pad], axis=0).T
    k_t = jnp.concatenate([k, pad], axis=0).T
    logi, logf, m0 = li_ref[0], lf_ref[0], m_ref[0]
    g = logf + m0
    m = jnp.maximum(g, logi)
    w = jnp.exp(logi - m)
    inter = jnp.exp(g - m)
    floor = jnp.exp(-m)
    for h in range(n_heads):
        c0, n0 = c_ref[0, h], n_ref[0, h]
        qh, kh, vh = q[h:h + 1], k[h:h + 1], v[h:h + 1]
        wh, ih = w[h:h + 1], inter[h:h + 1]
        s = jnp.sum(qh * kh, axis=1, keepdims=True) * wh
        num = s * vh + ih * jnp.sum(q_t[:, h:h + 1] * c0, axis=0, keepdims=True)
        den = s + ih * jnp.sum(qh * n0, axis=1, keepdims=True)
        hh = num / jnp.maximum(jnp.abs(den), floor[h:h + 1])
        co_ref[0, h] = ih * c0 + (wh * k_t[:, h:h + 1]) * vh
        no_ref[0, h] = ih * n0 + wh * kh
        hn = hh * lax.rsqrt(jnp.mean(hh * hh, axis=-1, keepdims=True) + EPS) * g_ref[h:h + 1]
        hm_ref[0, h:h + 1] = hn * jax.nn.sigmoid(og_ref[0, h:h + 1])
    mo_ref[0] = m


def mlstm_step(q, k, v, og, mh_g, logi, logf, c0, n0, m0):
    nb, n_heads, dqk = q.shape
    dv = v.shape[2]
    b3 = lambda a: pl.BlockSpec((1,) + a.shape[1:], lambda i: (i,) + (0,) * (a.ndim - 1))
    outs = [jax.ShapeDtypeStruct((nb, n_heads, dv), F32), jax.ShapeDtypeStruct(c0.shape, F32),
            jax.ShapeDtypeStruct(n0.shape, F32), jax.ShapeDtypeStruct(m0.shape, F32)]
    return pl.pallas_call(
        functools.partial(_mlstm_step_kernel, n_heads=n_heads), grid=(nb,),
        in_specs=[b3(q), b3(k), b3(v), b3(og), pl.BlockSpec(mh_g.shape, lambda i: (0, 0)),
                  b3(logi), b3(logf), b3(c0), b3(n0), b3(m0)],
        out_specs=[b3(o) for o in outs], out_shape=outs,
        compiler_params=_cparams(("parallel",)), name="mlstm_step")(q, k, v, og, mh_g, logi, logf, c0, n0, m0)


def _fox_kernel(q_ref, k_ref, v_ref, km_ref, vm_ref, cf_ref, cft_ref, cmt_ref, o_ref, kb, vb,
                *, tq, tk, n_meta, scale, lane0, causal_meta, n_kv_max):
    h, qi = pl.program_id(1), pl.program_id(2)

    if n_kv_max:
        @pl.when(qi == 0)
        def _():
            kb[...] = k_ref[...].astype(BF16)
            vb[...] = v_ref[...].astype(BF16)

    q = q_ref[...]
    lane = lax.broadcasted_iota(jnp.int32, (tq, LANES), 1)
    fq = jnp.sum(jnp.where(lane == lane0 + h, cf_ref[...], 0.0), axis=1, keepdims=True)

    s = lax.dot_general(q, km_ref[...].astype(BF16), NT_DIMS, preferred_element_type=F32) * scale + fq - cmt_ref[...]
    col = lax.broadcasted_iota(jnp.int32, s.shape, 1)
    ok = col < n_meta
    if causal_meta:
        ok = ok & (col <= lax.broadcasted_iota(jnp.int32, s.shape, 0))
    s = jnp.where(ok, s, -jnp.inf)
    m = jnp.max(s, axis=1, keepdims=True)
    p = jnp.exp(s - m)
    l = jnp.sum(p, axis=1, keepdims=True)
    acc = jnp.dot(p.astype(BF16), vm_ref[...].astype(BF16), preferred_element_type=F32)

    def body(j, carry):
        m, l, acc = carry
        start = pl.multiple_of(j * tk, tk)
        s = (lax.dot_general(q, kb[pl.ds(start, tk), :], NT_DIMS, preferred_element_type=F32) * scale
             + fq - cft_ref[pl.ds(j, 1), :])
        rows = qi * tq + lax.broadcasted_iota(jnp.int32, s.shape, 0)
        cols = j * tk + lax.broadcasted_iota(jnp.int32, s.shape, 1)
        s = jnp.where(cols <= rows, s, -jnp.inf)
        mn = jnp.maximum(m, jnp.max(s, axis=1, keepdims=True))
        alpha = jnp.exp(m - mn)
        p = jnp.exp(s - mn)
        l = alpha * l + jnp.sum(p, axis=1, keepdims=True)
        acc = alpha * acc + jnp.dot(p.astype(BF16), vb[pl.ds(start, tk), :], preferred_element_type=F32)
        return mn, l, acc

    if n_kv_max:
        m, l, acc = lax.fori_loop(0, ((qi + 1) * tq + tk - 1) // tk, body, (m, l, acc))
    o_ref[...] = (acc / l).astype(o_ref.dtype)


def fox_prompt(q, k, v, k_meta, v_meta, cf, cft, cmt, n_seq, n_meta, lane0, scale, causal_meta):
    r, width = q.shape
    dh = LANES
    n_heads = width // dh
    seq = r // n_seq
    tq = tk = min(256, seq)
    nq = seq // tq
    s_rows = k_meta.shape[0]
    n_kv = 0 if causal_meta else seq // tk
    cft3 = cft.reshape(LANES, r // tk, tk)
    cmt3 = cmt.reshape(LANES, 1, s_rows)
    kern = functools.partial(_fox_kernel, tq=tq, tk=tk, n_meta=n_meta, scale=scale, lane0=lane0,
                             causal_meta=causal_meta, n_kv_max=n_kv)
    kv_block = (seq, dh)
    return pl.pallas_call(
        kern, grid=(n_seq, n_heads, nq),
        in_specs=[pl.BlockSpec((tq, dh), lambda b, h, i: (b * nq + i, h)),
                  pl.BlockSpec(kv_block, lambda b, h, i: (b, h)),
                  pl.BlockSpec(kv_block, lambda b, h, i: (b, h)),
                  pl.BlockSpec((s_rows, dh), lambda b, h, i: (0, h)),
                  pl.BlockSpec((s_rows, dh), lambda b, h, i: (0, h)),
                  pl.BlockSpec((tq, LANES), lambda b, h, i: (b * nq + i, 0)),
                  pl.BlockSpec((None, seq // tk, tk), lambda b, h, i: (lane0 + h, b, 0)),
                  pl.BlockSpec((None, 1, s_rows), lambda b, h, i: (lane0 + h, 0, 0))],
        out_specs=pl.BlockSpec((tq, dh), lambda b, h, i: (b * nq + i, h)),
        out_shape=jax.ShapeDtypeStruct((r, width), BF16),
        scratch_shapes=[pltpu.VMEM(kv_block, BF16), pltpu.VMEM(kv_block, BF16)],
        compiler_params=_cparams(("parallel", "parallel", "arbitrary")), name="fox_prompt")(
            q, k, v, k_meta, v_meta, cf, cft3, cmt3)


def _largest_divisor(n, cap):
    return max(k for k in range(1, cap + 1) if n % k == 0)


def _page_logs_kernel(x_ref, r_ref, t_ref, *, n_heads, rpp, sub_pages):
    hi = lax.Precision.HIGHEST
    li = lax.broadcasted_iota(jnp.int32, (LANES, LANES), 0)
    lo = lax.broadcasted_iota(jnp.int32, (LANES, LANES), 1)
    same_head = (li % n_heads) == (lo % n_heads)
    w_row = (same_head & (li // n_heads >= lo // n_heads)).astype(F32)
    w_all = same_head.astype(F32)
    sb = sub_pages * rpp
    ri = lax.broadcasted_iota(jnp.int32, (sb, sb), 0)
    ro = lax.broadcasted_iota(jnp.int32, (sb, sb), 1)
    later_rows = ((ro > ri) & (ro // rpp == ri // rpp)).astype(F32)
    pi = lax.broadcasted_iota(jnp.int32, (sub_pages * 8, sb), 0)
    po = lax.broadcasted_iota(jnp.int32, (sub_pages * 8, sb), 1)
    page_rows = (po // rpp == pi // 8).astype(F32)
    for s in range(x_ref.shape[0] // sb):
        x = x_ref[s * sb:(s + 1) * sb, :]
        row_tot = jnp.dot(x, w_all, precision=hi, preferred_element_type=F32)
        r_ref[s * sb:(s + 1) * sb, :] = (jnp.dot(x, w_row, precision=hi, preferred_element_type=F32)
                                         + jnp.dot(later_rows, row_tot, precision=hi, preferred_element_type=F32) - x)
        t_ref[s * sub_pages * 8:(s + 1) * sub_pages * 8, :] = jnp.dot(page_rows, row_tot, precision=hi,
                                                                      preferred_element_type=F32)


def page_logs(cache_lf):
    depth, n_pool, page, n_heads = cache_lf.shape
    assert LANES % n_heads == 0 and (page * n_heads) % (8 * LANES) == 0
    rpp = page * n_heads // LANES
    n_pages = depth * n_pool
    sub_pages = _largest_divisor(n_pages, max(1, LANES // rpp))
    tile_pages = sub_pages * _largest_divisor(n_pages // sub_pages, 8)
    x = cache_lf.reshape(n_pages * rpp, LANES)
    kern = functools.partial(_page_logs_kernel, n_heads=n_heads, rpp=rpp, sub_pages=sub_pages)
    r, t = pl.pallas_call(
        kern, grid=(n_pages // tile_pages,),
        in_specs=[pl.BlockSpec((tile_pages * rpp, LANES), lambda i: (i, 0))],
        out_specs=[pl.BlockSpec((tile_pages * rpp, LANES), lambda i: (i, 0)),
                   pl.BlockSpec((tile_pages * 8, LANES), lambda i: (i, 0))],
        out_shape=[jax.ShapeDtypeStruct((n_pages * rpp, LANES), F32), jax.ShapeDtypeStruct((n_pages * 8, LANES), F32)],
        compiler_params=_cparams(("parallel",)), name="page_logs")(x)
    return r.reshape(depth, n_pool, rpp, LANES), t.reshape(depth, n_pool, 8, LANES)


def _fox_decode_kernel(pt_ref, q_ref, kn_ref, vn_ref, fq_ref, k_ref, v_ref, r_ref, t_ref, o_ref,
                       m_s, l_s, acc, carry, *, n_heads, scale):
    p = pl.program_id(1)
    q = q_ref[...]

    @pl.when(p == 0)
    def _():
        m_s[...] = jnp.sum(q.astype(F32) * kn_ref[...], axis=1, keepdims=True) * scale
        l_s[...] = jnp.ones_like(l_s)
        acc[...] = vn_ref[...]
        carry[...] = jnp.zeros_like(carry)

    later = r_ref[...] + carry[0:1, :]
    carry[...] = carry[...] + t_ref[...]
    bias = jnp.concatenate([jnp.broadcast_to(later[c:c + 1], (n_heads, LANES)) for c in range(later.shape[0])], axis=1)
    s = lax.dot_general(q, k_ref[...].astype(BF16), NT_DIMS, preferred_element_type=F32) * scale + fq_ref[...] + bias
    own = (lax.broadcasted_iota(jnp.int32, s.shape, 1) % n_heads) == lax.broadcasted_iota(jnp.int32, s.shape, 0)
    s = jnp.where(own, s, -jnp.inf)
    mn = jnp.maximum(m_s[...], jnp.max(s, axis=1, keepdims=True))
    alpha = jnp.exp(m_s[...] - mn)
    pr = jnp.exp(s - mn)
    l_s[...] = alpha * l_s[...] + jnp.sum(pr, axis=1, keepdims=True)
    acc[...] = alpha * acc[...] + jnp.dot(pr.astype(BF16), v_ref[...].astype(BF16), preferred_element_type=F32)
    m_s[...] = mn

    @pl.when(p == pl.num_programs(1) - 1)
    def _():
        o_ref[...] = acc[...] / l_s[...]


def fox_decode(page_table, layer, q, k_new, v_new, fq, cache_k, cache_v, later_logs, page_totals, scale):
    nb, n_pages = page_table.shape
    depth, n_pool, page, n_heads, dh = cache_k.shape
    rows = page * n_heads
    ck = cache_k.reshape(depth, n_pool, rows, dh)
    cv = cache_v.reshape(depth, n_pool, rows, dh)
    kern = functools.partial(_fox_decode_kernel, n_heads=n_heads, scale=scale)
    per_b = lambda w: pl.BlockSpec((None, n_heads, w), lambda i, p, pt: (i, 0, 0))
    paged = lambda a: pl.BlockSpec((None, None) + a.shape[2:],
                                   lambda i, p, pt: (layer, pt[i, n_pages - 1 - p], 0, 0))
    grid_spec = pltpu.PrefetchScalarGridSpec(
        num_scalar_prefetch=1, grid=(nb, n_pages),
        in_specs=[per_b(dh), per_b(dh), per_b(dh), per_b(1), paged(ck), paged(cv), paged(later_logs),
                  paged(page_totals)],
        out_specs=per_b(dh),
        scratch_shapes=[pltpu.VMEM((n_heads, 1), F32), pltpu.VMEM((n_heads, 1), F32), pltpu.VMEM((n_heads, dh), F32),
                        pltpu.VMEM((8, LANES), F32)])
    return pl.pallas_call(
        kern, grid_spec=grid_spec, out_shape=jax.ShapeDtypeStruct((nb, n_heads, dh), F32),
        compiler_params=_cparams(("arbitrary", "arbitrary")), name="fox_decode")(
            page_table, q, k_new, v_new, fq, ck, cv, later_logs, page_totals)


def kernel(x_prompt, x_sample, cache_k, cache_v, cache_lf, state_c, state_n, state_m, page_table, meta, norm1_g,
           w_in, b_in, q_norm_g, k_norm_g, mh_norm_g, p_a, p_b, w_out, norm2_g, w_gate, w_up, w_down):
    nb, seq, d = x_prompt.shape
    db = x_sample.shape[0]
    assert x_sample.shape[1] == 1
    n_meta = meta.shape[0]
    depth = w_in.shape[0]
    _, _, hm, dqk, dv = state_c.shape
    _, _, _, hf, dh = cache_k.shape
    d_ff = w_gate.shape[2]
    assert dh == LANES and dqk == LANES and seq % CHUNK == 0 and n_meta + db <= SMALL_ROWS
    assert 2 * hm + hf <= LANES
    r = nb * seq
    sizes = (hm * dqk, hm * dqk, hm * dv, hm * dv, hm, hm, hf * dh, hf * dh, hf * dh, hf, d, d)
    starts = [0]
    for sz in sizes:
        starts.append(starts[-1] + sz)
    seg = lambda a, i: lax.slice_in_dim(a, starts[i], starts[i + 1], axis=a.ndim - 1)
    order = (0, 1, 2, 6, 7, 8, 3, 10, 11, 4, 5, 9)
    n_gate = 2 * hm + hf
    off_q = 2 * hm * dqk + hm * dv
    off_k, off_v, off_g = off_q + hf * dh, off_q + 2 * hf * dh, off_q + 3 * hf * dh
    off_gates = off_g + hm * dv + 2 * d
    ff_pad = -d_ff % 512
    tm, tn = 512, 512
    scale_f = dh ** -0.5
    sample = slice(n_meta, n_meta + db)

    xb = x_prompt.reshape(r, d)
    xs = jnp.concatenate([meta.astype(F32), x_sample.reshape(db, d),
                          jnp.zeros((SMALL_ROWS - n_meta - db, d), F32)], axis=0)
    qk_scale = jnp.concatenate([jnp.ones((1, hm * dqk), F32), jnp.full((1, hm * dqk), dqk ** -0.5, F32),
                                jnp.ones((1, hm * dv), F32)], axis=1)
    zero_row = jnp.zeros((1, LANES), F32)
    later_logs, page_totals = page_logs(cache_lf)
    outs = [[] for _ in range(12)]

    for l in range(depth):
        wp = jnp.concatenate([seg(w_in[l], i) for i in order] + [jnp.zeros((d, LANES - n_gate), F32)], axis=1).astype(BF16)
        bp = jnp.concatenate([seg(b_in[l], i) for i in order] + [jnp.zeros((LANES - n_gate,), F32)]).reshape(1, -1)
        pa, pb, wo = p_a[l].astype(BF16), p_b[l].astype(BF16), w_out[l].astype(BF16)
        wg = jnp.pad(w_gate[l], ((0, 0), (0, ff_pad))).astype(BF16)
        wu = jnp.pad(w_up[l], ((0, 0), (0, ff_pad))).astype(BF16)
        wd = jnp.pad(w_down[l], ((0, ff_pad), (0, 0))).astype(BF16)

        hb, hs = rmsnorm_rows(xb, norm1_g[l]), rmsnorm_rows(xs, norm1_g[l])
        lhs = [(hb, hs)]
        qkv, qkv_s = fused_matmul(lhs, [(wp, 0, 0, bp)], [], [qk_scale], _epi_scale, off_q, BF16, tm, tn, "in_qkv")
        fq, fq_s = fused_matmul(lhs, [(wp, 0, off_q, bp)], [], [q_norm_g[l].reshape(1, dh)], _epi_headnorm,
                                hf * dh, BF16, tm, tn, "in_fq")
        fk, fk_s = fused_matmul(lhs, [(wp, 0, off_k, bp)], [], [k_norm_g[l].reshape(1, dh)], _epi_headnorm,
                                hf * dh, F32, tm, tn, "in_fk")
        fv, fv_s = fused_matmul(lhs, [(wp, 0, off_v, bp)], [], [], _epi_identity, hf * dh, F32, tm, tn, "in_fv")
        gts, gts_s = fused_matmul(lhs, [(wp, 0, off_g, bp)], [], [], _epi_identity, hm * dv + 2 * d, F32, tm, tn,
                                  "in_gates")
        ls, cbs, _, lst, cbst, _ = gate_logs(hs, wp, off_gates, bp, zero_row, 1, hm, n_gate)
        lb, cbb, cfb, lbt, cbbt, cfbt = gate_logs(hb, wp, off_gates, bp, cbs[n_meta - 1:n_meta], nb, hm, n_gate)

        mh_g = mh_norm_g[l].reshape(1, hm * dv)
        hm_meta, c_m, n_m, m_m = mlstm_chunks(
            qkv_s, gts_s, 0, mh_g, ls, cbs, lst, cbst, jnp.zeros((hm, dqk, dv), F32), jnp.zeros((hm, 1, dqk), F32),
            jnp.zeros((hm, 1, LANES), F32), 1, n_meta)
        hm_b, c_p, n_p, m_p = mlstm_chunks(qkv, gts, 0, mh_g, lb, cbb, lbt, cbbt, c_m[0], n_m[0], m_m[0], nb, CHUNK)
        hm_d, c_d, n_d, m_d = mlstm_step(
            qkv_s[sample, :hm * dqk].reshape(db, hm, dqk), qkv_s[sample, hm * dqk:2 * hm * dqk].reshape(db, hm, dqk),
            qkv_s[sample, 2 * hm * dqk:].reshape(db, hm, dv), gts_s[sample, :hm * dv].reshape(db, hm, dv),
            mh_g.reshape(hm, dv), ls[sample, :hm].reshape(db, hm, 1), ls[sample, hm:2 * hm].reshape(db, hm, 1),
            state_c[l], state_n[l].reshape(db, hm, 1, dqk), state_m[l].reshape(db, hm, 1))
        pad_rows = jnp.zeros((SMALL_ROWS - n_meta - db, hm * dv), BF16)
        hm_s = jnp.concatenate([hm_meta[:n_meta], hm_d.reshape(db, hm * dv).astype(BF16), pad_rows], axis=0)

        hf_meta = fox_prompt(fq_s, fk_s, fv_s, fk_s, fv_s, cbs, cbst, cbst, 1, n_meta, 2 * hm, scale_f, True)
        hf_b = fox_prompt(fq, fk, fv, fk_s, fv_s, cfb, cfbt, cbst, nb, n_meta, 2 * hm, scale_f, False)
        hf_d = fox_decode(page_table, l, fq_s[sample].reshape(db, hf, dh), fk_s[sample].reshape(db, hf, dh),
                          fv_s[sample].reshape(db, hf, dh), ls[sample, 2 * hm:n_gate].reshape(db, hf, 1),
                          cache_k, cache_v, later_logs, page_totals, scale_f)
        hf_s = jnp.concatenate([hf_meta[:n_meta], hf_d.reshape(db, hf * dh).astype(BF16), pad_rows], axis=0)

        u, u_s = fused_matmul([(hm_b, hm_s), (hf_b, hf_s)], [(pa, 0, 0, None), (pb, 1, 0, None)],
                              [(gts, gts_s, hm * dv), (gts, gts_s, hm * dv + d)], [], _epi_gated2, d, BF16, tm, tn,
                              "merge_gate")
        x1, x1_s = fused_matmul([(u, u_s)], [(wo, 0, 0, None)], [(xb, xs, 0)], [], _epi_residual, d, F32, tm, tn,
                                "out_proj")
        h2, h2_s = rmsnorm_rows(x1, norm2_g[l]), rmsnorm_rows(x1_s, norm2_g[l])
        act, act_s = fused_matmul([(h2, h2_s)], [(wg, 0, 0, None), (wu, 0, 0, None)], [], [], _epi_swiglu,
                                  d_ff + ff_pad, BF16, tm, tn, "ffn_up")
        xb, xs = fused_matmul([(act, act_s)], [(wd, 0, 0, None)], [(x1, x1_s, 0)], [], _epi_residual, d, F32, tm, 256,
                              "ffn_down")

        with_meta = lambda small, big, tail: jnp.concatenate(
            [jnp.broadcast_to(small[None, :n_meta], (nb, n_meta) + tail), big.reshape((nb, seq) + tail)],
            axis=1)
        outs[0].append(with_meta(fk_s.reshape(SMALL_ROWS, hf, dh), fk, (hf, dh)))
        outs[1].append(with_meta(fv_s.reshape(SMALL_ROWS, hf, dh), fv, (hf, dh)))
        outs[2].append(with_meta(ls[:, 2 * hm:n_gate], lb[:, 2 * hm:n_gate], (hf,)))
        outs[3].append(c_p)
        outs[4].append(n_p.reshape(nb, hm, dqk))
        outs[5].append(m_p[:, :, 0, 0])
        outs[6].append(fk_s[sample].reshape(db, 1, hf, dh))
        outs[7].append(fv_s[sample].reshape(db, 1, hf, dh))
        outs[8].append(ls[sample, 2 * hm:n_gate].reshape(db, 1, hf))
        outs[9].append(c_d)
        outs[10].append(n_d.reshape(db, hm, dqk))
        outs[11].append(m_d.reshape(db, hm))

    return (xb.reshape(nb, seq, d), xs[sample].reshape(db, 1, d)) + tuple(jnp.stack(o) for o in outs)
```

```python
import functools

import jax
import jax.numpy as jnp
from jax import lax
from jax.experimental import pallas as pl
from jax.experimental.pallas import tpu as pltpu

F32 = jnp.float32
BF16 = jnp.bfloat16
EPS = 1e-6
LANES = 128
CHUNK = 128
SMALL_ROWS = 128
VMEM_LIMIT = 52 * 1024 * 1024
NT_DIMS = (((1,), (1,)), ((), ()))


def _cparams(sem):
    return pltpu.CompilerParams(dimension_semantics=sem, vmem_limit_bytes=VMEM_LIMIT)


def _largest_divisor(n, cap, multiple_of=1):
    return max(k for k in range(multiple_of, cap + 1, multiple_of) if n % k == 0)


def _cast_kernel(*refs, shift, valid_rows, valid_cols):
    if shift:
        a_ref, b_ref, o_ref = refs
        wide = jnp.concatenate([a_ref[...], b_ref[...]], axis=1)
        x = pltpu.roll(wide, wide.shape[1] - shift, 1)[:, :a_ref.shape[1]]
    else:
        a_ref, o_ref = refs
        x = a_ref[...]
    tr, tc = x.shape
    if valid_rows is not None:
        rows = pl.program_id(1) * tr + lax.broadcasted_iota(jnp.int32, x.shape, 0)
        x = jnp.where(rows < valid_rows, x, 0.0)
    if valid_cols is not None:
        cols = pl.program_id(2) * tc + lax.broadcasted_iota(jnp.int32, x.shape, 1)
        x = jnp.where(cols < valid_cols, x, 0.0)
    o_ref[...] = x.astype(o_ref.dtype)


def cast_weights(w, col_start, n_cols, rows_out, cols_out, name):
    depth, rows, cols = w.shape
    shift = col_start % LANES
    base = col_start - shift
    tr = _largest_divisor(rows_out, 1024, 8)
    tc = _largest_divisor(cols_out, 2816, LANES)
    assert base % tc == 0 and (not shift or cols_out == n_cols)
    in_specs = [pl.BlockSpec((None, tr, tc), lambda l, i, j, o=base // tc: (l, i, j + o))]
    args = [w]
    if shift:
        in_specs.append(pl.BlockSpec((None, tr, LANES), lambda l, i, j, o=base // LANES, s=tc // LANES: (l, i, o + (j + 1) * s)))
        args.append(w)
    kern = functools.partial(_cast_kernel, shift=shift, valid_rows=rows if rows_out != rows else None,
                             valid_cols=n_cols if cols_out != n_cols else None)
    return pl.pallas_call(
        kern, grid=(depth, rows_out // tr, cols_out // tc), in_specs=in_specs,
        out_specs=pl.BlockSpec((None, tr, tc), lambda l, i, j: (l, i, j)),
        out_shape=jax.ShapeDtypeStruct((depth, rows_out, cols_out), BF16),
        compiler_params=_cparams(("parallel", "parallel", "parallel")), name=name)(*args)


def _norm_kernel(x_ref, g_ref, o_ref):
    x = x_ref[...]
    y = x * lax.rsqrt(jnp.mean(x * x, axis=-1, keepdims=True) + EPS)
    o_ref[...] = (y * g_ref[...]).astype(o_ref.dtype)


def rmsnorm_rows(x, g):
    m, d = x.shape
    tm = min(256, m)
    return pl.pallas_call(
        _norm_kernel, grid=(m // tm,),
        in_specs=[pl.BlockSpec((tm, d), lambda i: (i, 0)), pl.BlockSpec((1, d), lambda i: (0, 0))],
        out_specs=pl.BlockSpec((tm, d), lambda i: (i, 0)),
        out_shape=jax.ShapeDtypeStruct((m, d), BF16),
        compiler_params=_cparams(("parallel",)), name="rmsnorm")(x, g.reshape(1, d))


def _mm_kernel(*refs, n_lhs, ws, n_extra, n_const, epi):
    it = iter(refs)
    lhs_big = [next(it) for _ in range(n_lhs)]
    lhs_small = [next(it) for _ in range(n_lhs)]
    w_refs = [next(it) for _ in ws]
    b_refs = [next(it) if has_bias else None for (_, has_bias) in ws]
    ex_big = [next(it) for _ in range(n_extra)]
    ex_small = [next(it) for _ in range(n_extra)]
    consts = [next(it) for _ in range(n_const)]
    out_big = next(it)
    out_small = next(it)

    def run(lhs, extras, out):
        accs = []
        for k, (li, _) in enumerate(ws):
            a = jnp.dot(lhs[li][...], w_refs[k][...], preferred_element_type=F32)
            if b_refs[k] is not None:
                a = a + b_refs[k][...]
            accs.append(a)
        out[...] = epi(accs, [e[...] for e in extras], [c[...] for c in consts]).astype(out.dtype)

    run(lhs_big, ex_big, out_big)

    @pl.when(pl.program_id(1) == 0)
    def _():
        run(lhs_small, ex_small, out_small)


def fused_matmul(lhs, layer, weights, extras, consts, epi, n_cols, out_dtype, tm, tn, name, single_buffer_w=False):
    r = lhs[0][0].shape[0]
    s = lhs[0][1].shape[0]
    tm = min(tm, r)
    assert r % tm == 0 and n_cols % tn == 0
    in_specs, args = [], []
    for big, _ in lhs:
        in_specs.append(pl.BlockSpec((tm, big.shape[1]), lambda j, i: (i, 0)))
        args.append(big)
    for _, small in lhs:
        in_specs.append(pl.BlockSpec(small.shape, lambda j, i: (0, 0)))
        args.append(small)
    w_mode = dict(pipeline_mode=pl.Buffered(1)) if single_buffer_w else {}
    for w, _, off, _ in weights:
        assert off % tn == 0
        in_specs.append(pl.BlockSpec((None, w.shape[1], tn), lambda j, i, o=off // tn: (layer, 0, j + o), **w_mode))
        args.append(w)
    for _, _, _, b in weights:
        if b is not None:
            assert b.shape == (1, n_cols)
            in_specs.append(pl.BlockSpec((1, tn), lambda j, i: (0, j)))
            args.append(b)
    for big, _, off in extras:
        assert off % tn == 0
        in_specs.append(pl.BlockSpec((tm, tn), lambda j, i, o=off // tn: (i, j + o)))
        args.append(big)
    for _, small, off in extras:
        in_specs.append(pl.BlockSpec((s, tn), lambda j, i, o=off // tn: (0, j + o)))
        args.append(small)
    for c in consts:
        if c.shape == (1, n_cols):
            in_specs.append(pl.BlockSpec((1, tn), lambda j, i: (0, j)))
        else:
            in_specs.append(pl.BlockSpec(c.shape, lambda j, i, nd=c.ndim: (0,) * nd))
        args.append(c)
    kern = functools.partial(
        _mm_kernel, n_lhs=len(lhs), ws=[(li, b is not None) for _, li, _, b in weights],
        n_extra=len(extras), n_const=len(consts), epi=epi)
    return pl.pallas_call(
        kern, grid=(n_cols // tn, r // tm), in_specs=in_specs,
        out_specs=[pl.BlockSpec((tm, tn), lambda j, i: (i, j)), pl.BlockSpec((s, tn), lambda j, i: (0, j))],
        out_shape=[jax.ShapeDtypeStruct((r, n_cols), out_dtype), jax.ShapeDtypeStruct((s, n_cols), out_dtype)],
        compiler_params=_cparams(("parallel", "arbitrary")), name=name)(*args)


def _epi_scale(accs, extras, consts):
    return accs[0] * consts[0]


def _epi_identity(accs, extras, consts):
    return accs[0]


def _epi_headnorm(accs, extras, consts):
    y, g = accs[0], consts[0]
    outs = []
    for h in range(y.shape[1] // LANES):
        yh = y[:, h * LANES:(h + 1) * LANES]
        outs.append(yh * lax.rsqrt(jnp.mean(yh * yh, axis=-1, keepdims=True) + EPS) * g)
    return jnp.concatenate(outs, axis=-1)


def _epi_gated2(accs, extras, consts):
    return (jax.nn.sigmoid(extras[0].astype(F32)) * accs[0] + jax.nn.sigmoid(extras[1].astype(F32)) * accs[1])


def _epi_residual(accs, extras, consts):
    return extras[0] + accs[0]


def _epi_swiglu(accs, extras, consts):
    return jax.nn.silu(accs[0]) * accs[1]


def _gates_kernel(h_ref, w_ref, b_ref, c0_ref, l_ref, cb_ref, cf_ref, lt_ref, cbt_ref, cft_ref, carry,
                  *, n_lin, n_gate, cs):
    @pl.when(pl.program_id(1) == 0)
    def _():
        carry[...] = c0_ref[...]

    z = jnp.dot(h_ref[...], w_ref[...], preferred_element_type=F32) + b_ref[...]
    lane = lax.broadcasted_iota(jnp.int32, z.shape, 1)
    logs = jnp.where(lane < n_lin, z, jnp.where(lane < n_gate, jax.nn.log_sigmoid(z), 0.0))
    tri = (lax.broadcasted_iota(jnp.int32, (cs, cs), 1) <= lax.broadcasted_iota(jnp.int32, (cs, cs), 0)).astype(F32)
    for s in range(z.shape[0] // cs):
        rows = slice(s * cs, (s + 1) * cs)
        lc = logs[rows]
        cb = jnp.dot(tri, lc, precision=lax.Precision.HIGHEST, preferred_element_type=F32)
        cf = cb + carry[...]
        carry[...] = cf[cs - 1:cs, :]
        l_ref[rows, :] = lc
        cb_ref[rows, :] = cb
        cf_ref[rows, :] = cf
        lt_ref[:, rows] = lc.T
        cbt_ref[:, rows] = cb.T
        cft_ref[:, rows] = cf.T


def gate_logs(h, w, layer, bias, carry0, n_seq, n_lin, n_gate):
    r, d = h.shape
    seq = r // n_seq
    tile = min(512, seq)
    nt = seq // tile
    cs = min(CHUNK, tile)
    row = pl.BlockSpec((tile, LANES), lambda b, t: (b * nt + t, 0))
    col = pl.BlockSpec((LANES, tile), lambda b, t: (0, b * nt + t))
    kern = functools.partial(_gates_kernel, n_lin=n_lin, n_gate=n_gate, cs=cs)
    return pl.pallas_call(
        kern, grid=(n_seq, nt),
        in_specs=[pl.BlockSpec((tile, d), lambda b, t: (b * nt + t, 0)),
                  pl.BlockSpec((None, d, LANES), lambda b, t: (layer, 0, 0)),
                  pl.BlockSpec((1, LANES), lambda b, t: (0, 0)),
                  pl.BlockSpec((1, LANES), lambda b, t: (0, 0))],
        out_specs=[row, row, row, col, col, col],
        out_shape=[jax.ShapeDtypeStruct((r, LANES), F32)] * 3 + [jax.ShapeDtypeStruct((LANES, r), F32)] * 3,
        scratch_shapes=[pltpu.VMEM((1, LANES), F32)],
        compiler_params=_cparams(("arbitrary", "arbitrary")), name="gate_logs")(h, w, bias, carry0)


def _mlstm_head(q, k, v, bcol, brow, icol, irow, c0, n0, m0, nv):
    ln = q.shape[0]
    g = bcol + m0
    d = bcol - brow + irow
    ti = lax.broadcasted_iota(jnp.int32, (ln, ln), 0)
    si = lax.broadcasted_iota(jnp.int32, (ln, ln), 1)
    d = jnp.where(si <= ti, d, -jnp.inf)
    m = jnp.maximum(g, jnp.max(d, axis=1, keepdims=True))
    w = jnp.exp(d - m)
    inter = jnp.exp(g - m)
    s = lax.dot_general(q, k, NT_DIMS, preferred_element_type=F32) * w
    num = (jnp.dot(s.astype(BF16), v, preferred_element_type=F32)
           + inter * jnp.dot(q, c0.astype(BF16), preferred_element_type=F32))
    den = jnp.sum(s, axis=1, keepdims=True) + inter * jnp.sum(q.astype(F32) * n0, axis=1, keepdims=True)
    h = num / jnp.maximum(jnp.abs(den), jnp.exp(-m))
    m_end = m[nv - 1:nv]
    sidx = lax.broadcasted_iota(jnp.int32, (ln, 1), 0)
    w_end = jnp.where(sidx < nv, jnp.exp(bcol[nv - 1:nv] - bcol + icol - m_end), 0.0)
    decay = jnp.exp(g[nv - 1:nv] - m_end)
    kw = k.astype(F32) * w_end
    c1 = decay * c0 + jnp.dot(kw.T.astype(BF16), v, preferred_element_type=F32)
    n1 = decay * n0 + jnp.sum(kw, axis=0, keepdims=True)
    return h, c1, n1, m_end


def _mlstm_kernel(q_ref, k_ref, v_ref, og_ref, g_ref, l_ref, cb_ref, lt_ref, cbt_ref, ci_ref, ni_ref, mi_ref,
                  hm_ref, co_ref, no_ref, mo_ref, c_s, n_s, m_s, *, n_heads, dqk, dv, nv):
    c = pl.program_id(1)

    @pl.when(c == 0)
    def _():
        c_s[...] = ci_ref[...]
        n_s[...] = ni_ref[...]
        m_s[...] = mi_ref[...]

    logs, cb, logs_t, cb_t = l_ref[...], cb_ref[...], lt_ref[...], cbt_ref[...]
    for h in range(n_heads):
        hh, c1, n1, m_end = _mlstm_head(
            q_ref[:, h * dqk:(h + 1) * dqk], k_ref[:, h * dqk:(h + 1) * dqk], v_ref[:, h * dv:(h + 1) * dv],
            cb[:, n_heads + h:n_heads + h + 1], cb_t[n_heads + h:n_heads + h + 1, :],
            logs[:, h:h + 1], logs_t[h:h + 1, :], c_s[h], n_s[h], m_s[h][:, 0:1], nv)
        c_s[h] = c1
        n_s[h] = n1
        m_s[h] = jnp.broadcast_to(m_end, (1, LANES))
        cols = slice(h * dv, (h + 1) * dv)
        hn = hh * lax.rsqrt(jnp.mean(hh * hh, axis=-1, keepdims=True) + EPS) * g_ref[:, cols]
        hm_ref[:, cols] = (hn * jax.nn.sigmoid(og_ref[:, cols].astype(F32))).astype(hm_ref.dtype)

    @pl.when(c == pl.num_programs(1) - 1)
    def _():
        co_ref[0] = c_s[...]
        no_ref[0] = n_s[...]
        mo_ref[0] = m_s[...]


def mlstm_chunks(qkv, og, mh_g, logs, cb, logs_t, cb_t, c_init, n_init, m_init, n_seq, nv):
    r = qkv.shape[0]
    n_heads, dqk, dv = c_init.shape
    nc = r // n_seq // CHUNK
    hq, hv = n_heads * dqk, n_heads * dv
    assert hv == 2 * hq
    rows = lambda b, c: b * nc + c
    kern = functools.partial(_mlstm_kernel, n_heads=n_heads, dqk=dqk, dv=dv, nv=nv)
    full3 = lambda shape: pl.BlockSpec(shape, lambda b, c: (0, 0, 0))
    return pl.pallas_call(
        kern, grid=(n_seq, nc),
        in_specs=[pl.BlockSpec((CHUNK, hq), lambda b, c: (rows(b, c), 0)),
                  pl.BlockSpec((CHUNK, hq), lambda b, c: (rows(b, c), 1)),
                  pl.BlockSpec((CHUNK, hv), lambda b, c: (rows(b, c), 1)),
                  pl.BlockSpec((CHUNK, hv), lambda b, c: (rows(b, c), 0)),
                  pl.BlockSpec((1, hv), lambda b, c: (0, 0)),
                  pl.BlockSpec((CHUNK, LANES), lambda b, c: (rows(b, c), 0)),
                  pl.BlockSpec((CHUNK, LANES), lambda b, c: (rows(b, c), 0)),
                  pl.BlockSpec((LANES, CHUNK), lambda b, c: (0, rows(b, c))),
                  pl.BlockSpec((LANES, CHUNK), lambda b, c: (0, rows(b, c))),
                  full3(c_init.shape), full3(n_init.shape), full3(m_init.shape)],
        out_specs=[pl.BlockSpec((CHUNK, hv), lambda b, c: (rows(b, c), 0)),
                   pl.BlockSpec((1,) + c_init.shape, lambda b, c: (b, 0, 0, 0)),
                   pl.BlockSpec((1,) + n_init.shape, lambda b, c: (b, 0, 0, 0)),
                   pl.BlockSpec((1,) + m_init.shape, lambda b, c: (b, 0, 0, 0))],
        out_shape=[jax.ShapeDtypeStruct((r, hv), BF16),
                   jax.ShapeDtypeStruct((n_seq,) + c_init.shape, F32),
                   jax.ShapeDtypeStruct((n_seq,) + n_init.shape, F32),
                   jax.ShapeDtypeStruct((n_seq,) + m_init.shape, F32)],
        scratch_shapes=[pltpu.VMEM(c_init.shape, F32), pltpu.VMEM(n_init.shape, F32), pltpu.VMEM(m_init.shape, F32)],
        compiler_params=_cparams(("arbitrary", "arbitrary")), name="mlstm_chunks")(
            qkv, qkv, qkv, og, mh_g, logs, cb, logs_t, cb_t, c_init, n_init, m_init)


def _mlstm_step_kernel(q_ref, k_ref, v_ref, og_ref, g_ref, li_ref, lf_ref, c_ref, n_ref, m_ref,
                       hm_ref, co_ref, no_ref, mo_ref, *, n_heads):
    q, k = q_ref[0].astype(F32), k_ref[0].astype(F32)
    v = v_ref[0].astype(F32)
    og = og_ref[0].astype(F32)
    pad = jnp.zeros((LANES - n_heads, q.shape[1]), F32)
    q_t = jnp.concatenate([q, pad], axis=0).T
    k_t = jnp.concatenate([k, pad], axis=0).T
    logi, logf, m0 = li_ref[0], lf_ref[0], m_ref[0]
    g = logf + m0
    m = jnp.maximum(g, logi)
    w = jnp.exp(logi - m)
    inter = jnp.exp(g - m)
    floor = jnp.exp(-m)
    for h in range(n_heads):
        c0, n0 = c_ref[0, h], n_ref[0, h]
        qh, kh, vh = q[h:h + 1], k[h:h + 1], v[h:h + 1]
        wh, ih = w[h:h + 1], inter[h:h + 1]
        s = jnp.sum(qh * kh, axis=1, keepdims=True) * wh
        num = s * vh + ih * jnp.sum(q_t[:, h:h + 1] * c0, axis=0, keepdims=True)
        den = s + ih * jnp.sum(qh * n0, axis=1, keepdims=True)
        hh = num / jnp.maximum(jnp.abs(den), floor[h:h + 1])
        co_ref[0, h] = ih * c0 + (wh * k_t[:, h:h + 1]) * vh
        no_ref[0, h] = ih * n0 + wh * kh
        hn = hh * lax.rsqrt(jnp.mean(hh * hh, axis=-1, keepdims=True) + EPS) * g_ref[h:h + 1]
        hm_ref[0, h:h + 1] = hn * jax.nn.sigmoid(og[h:h + 1])
    mo_ref[0] = m


def mlstm_step(q, k, v, og, mh_g, logi, logf, c0, n0, m0):
    nb, n_heads, dqk = q.shape
    dv = v.shape[2]
    b3 = lambda a: pl.BlockSpec((1,) + a.shape[1:], lambda i: (i,) + (0,) * (a.ndim - 1))
    outs = [jax.ShapeDtypeStruct((nb, n_heads, dv), F32), jax.ShapeDtypeStruct(c0.shape, F32),
            jax.ShapeDtypeStruct(n0.shape, F32), jax.ShapeDtypeStruct(m0.shape, F32)]
    return pl.pallas_call(
        functools.partial(_mlstm_step_kernel, n_heads=n_heads), grid=(nb,),
        in_specs=[b3(q), b3(k), b3(v), b3(og), pl.BlockSpec(mh_g.shape, lambda i: (0, 0)),
                  b3(logi), b3(logf), b3(c0), b3(n0), b3(m0)],
        out_specs=[b3(o) for o in outs], out_shape=outs,
        compiler_params=_cparams(("parallel",)), name="mlstm_step")(q, k, v, og, mh_g, logi, logf, c0, n0, m0)


def _fox_kernel(q_ref, k_ref, v_ref, km_ref, vm_ref, cf_ref, cft_ref, cmt_ref, o_ref, kb, vb,
                *, tq, tk, n_meta, scale, lane0, causal_meta):
    h = pl.program_id(1)
    seq = q_ref.shape[0]
    if not causal_meta:
        kb[...] = k_ref[...].astype(BF16)
        vb[...] = v_ref[...].astype(BF16)
    km = km_ref[...].astype(BF16)
    vm = vm_ref[...].astype(BF16)
    lane = lax.broadcasted_iota(jnp.int32, (tq, LANES), 1)
    tri = lax.broadcasted_iota(jnp.int32, (tq, tk), 1) <= lax.broadcasted_iota(jnp.int32, (tq, tk), 0)

    def online(carry, s, vals):
        m, l, acc = carry
        mn = jnp.maximum(m, jnp.max(s, axis=1, keepdims=True))
        alpha = jnp.exp(m - mn)
        p = jnp.exp(s - mn)
        return (mn, alpha * l + jnp.sum(p, axis=1, keepdims=True),
                alpha * acc + jnp.dot(p.astype(BF16), vals, preferred_element_type=F32))

    for i in range(seq // tq):
        rows = slice(i * tq, (i + 1) * tq)
        q = q_ref[rows, :]
        fq = jnp.sum(jnp.where(lane == lane0 + h, cf_ref[rows, :], 0.0), axis=1, keepdims=True)
        s = lax.dot_general(q, km, NT_DIMS, preferred_element_type=F32) * scale + fq - cmt_ref[...]
        col = lax.broadcasted_iota(jnp.int32, s.shape, 1)
        ok = col < n_meta
        if causal_meta:
            ok = ok & (col <= lax.broadcasted_iota(jnp.int32, s.shape, 0))
        s = jnp.where(ok, s, -jnp.inf)
        m = jnp.max(s, axis=1, keepdims=True)
        p = jnp.exp(s - m)
        carry = (m, jnp.sum(p, axis=1, keepdims=True), jnp.dot(p.astype(BF16), vm, preferred_element_type=F32))
        if not causal_meta:
            for j in range(i + 1):
                cols = slice(j * tk, (j + 1) * tk)
                s = (lax.dot_general(q, kb[cols, :], NT_DIMS, preferred_element_type=F32) * scale
                     + fq - cft_ref[j:j + 1, :])
                if j == i:
                    s = jnp.where(tri, s, -jnp.inf)
                carry = online(carry, s, vb[cols, :])
        o_ref[rows, :] = (carry[2] / carry[1]).astype(o_ref.dtype)


def fox_prompt(q, k, v, k_meta, v_meta, cf, cft, cmt, n_seq, n_meta, lane0, scale, causal_meta):
    r, width = q.shape
    dh = LANES
    n_heads = width // dh
    seq = r // n_seq
    tq = tk = min(512, seq)
    s_rows = k_meta.shape[0]
    cft3 = cft.reshape(LANES, n_seq, seq // tk, tk)
    cmt3 = cmt.reshape(LANES, 1, s_rows)
    kern = functools.partial(_fox_kernel, tq=tq, tk=tk, n_meta=n_meta, scale=scale, lane0=lane0,
                             causal_meta=causal_meta)
    blk = pl.BlockSpec((seq, dh), lambda b, h: (b, h))
    return pl.pallas_call(
        kern, grid=(n_seq, n_heads),
        in_specs=[blk, blk, blk,
                  pl.BlockSpec((s_rows, dh), lambda b, h: (0, h)),
                  pl.BlockSpec((s_rows, dh), lambda b, h: (0, h)),
                  pl.BlockSpec((seq, LANES), lambda b, h: (b, 0)),
                  pl.BlockSpec((None, None, seq // tk, tk), lambda b, h: (lane0 + h, b, 0, 0)),
                  pl.BlockSpec((None, 1, s_rows), lambda b, h: (lane0 + h, 0, 0))],
        out_specs=blk,
        out_shape=jax.ShapeDtypeStruct((r, width), BF16),
        scratch_shapes=[pltpu.VMEM((seq, dh), BF16), pltpu.VMEM((seq, dh), BF16)],
        compiler_params=_cparams(("parallel", "arbitrary")), name="fox_prompt")(
            q, k, v, k_meta, v_meta, cf, cft3, cmt3)


def _page_logs_kernel(x_ref, r_ref, t_ref, *, n_heads, rpp, sub_pages):
    hi = lax.Precision.HIGHEST
    li = lax.broadcasted_iota(jnp.int32, (LANES, LANES), 0)
    lo = lax.broadcasted_iota(jnp.int32, (LANES, LANES), 1)
    same_head = (li % n_heads) == (lo % n_heads)
    w_row = (same_head & (li // n_heads >= lo // n_heads)).astype(F32)
    w_all = same_head.astype(F32)
    sb = sub_pages * rpp
    ri = lax.broadcasted_iota(jnp.int32, (sb, sb), 0)
    ro = lax.broadcasted_iota(jnp.int32, (sb, sb), 1)
    later_rows = ((ro > ri) & (ro // rpp == ri // rpp)).astype(F32)
    pi = lax.broadcasted_iota(jnp.int32, (sub_pages * 8, sb), 0)
    po = lax.broadcasted_iota(jnp.int32, (sub_pages * 8, sb), 1)
    page_rows = (po // rpp == pi // 8).astype(F32)
    for s in range(x_ref.shape[0] // sb):
        x = x_ref[s * sb:(s + 1) * sb, :]
        row_tot = jnp.dot(x, w_all, precision=hi, preferred_element_type=F32)
        r_ref[s * sb:(s + 1) * sb, :] = (jnp.dot(x, w_row, precision=hi, preferred_element_type=F32)
                                         + jnp.dot(later_rows, row_tot, precision=hi, preferred_element_type=F32) - x)
        t_ref[s * sub_pages * 8:(s + 1) * sub_pages * 8, :] = jnp.dot(page_rows, row_tot, precision=hi,
                                                                      preferred_element_type=F32)


def page_logs(cache_lf):
    depth, n_pool, page, n_heads = cache_lf.shape
    assert LANES % n_heads == 0 and (page * n_heads) % (8 * LANES) == 0
    rpp = page * n_heads // LANES
    n_pages = depth * n_pool
    sub_pages = _largest_divisor(n_pages, max(1, LANES // rpp))
    tile_pages = sub_pages * _largest_divisor(n_pages // sub_pages, 8)
    x = cache_lf.reshape(n_pages * rpp, LANES)
    kern = functools.partial(_page_logs_kernel, n_heads=n_heads, rpp=rpp, sub_pages=sub_pages)
    r, t = pl.pallas_call(
        kern, grid=(n_pages // tile_pages,),
        in_specs=[pl.BlockSpec((tile_pages * rpp, LANES), lambda i: (i, 0))],
        out_specs=[pl.BlockSpec((tile_pages * rpp, LANES), lambda i: (i, 0)),
                   pl.BlockSpec((tile_pages * 8, LANES), lambda i: (i, 0))],
        out_shape=[jax.ShapeDtypeStruct((n_pages * rpp, LANES), F32), jax.ShapeDtypeStruct((n_pages * 8, LANES), F32)],
        compiler_params=_cparams(("parallel",)), name="page_logs")(x)
    return r.reshape(depth, n_pool, rpp, LANES), t.reshape(depth, n_pool, 8, LANES)


def _fox_decode_kernel(pt_ref, q_ref, kn_ref, vn_ref, fq_ref, *refs, n_heads, scale, pages_per_step):
    page_refs = refs[:4 * pages_per_step]
    o_ref, m_s, l_s, acc, carry = refs[4 * pages_per_step:]
    p = pl.program_id(1)
    q = q_ref[...]

    @pl.when(p == 0)
    def _():
        m_s[...] = jnp.sum(q.astype(F32) * kn_ref[...], axis=1, keepdims=True) * scale
        l_s[...] = jnp.ones_like(l_s)
        acc[...] = vn_ref[...]
        carry[...] = jnp.zeros_like(carry)

    fq = fq_ref[...]
    later_pages = carry[0:1, :]
    m, l, a = m_s[...], l_s[...], acc[...]
    for u in range(pages_per_step):
        k_ref, v_ref, r_ref, t_ref = page_refs[4 * u:4 * u + 4]
        later = r_ref[...] + later_pages
        later_pages = later_pages + t_ref[0:1, :]
        bias = jnp.concatenate([jnp.broadcast_to(later[c:c + 1], (n_heads, LANES)) for c in range(later.shape[0])],
                               axis=1)
        s = lax.dot_general(q, k_ref[...].astype(BF16), NT_DIMS, preferred_element_type=F32) * scale + fq + bias
        own = (lax.broadcasted_iota(jnp.int32, s.shape, 1) % n_heads) == lax.broadcasted_iota(jnp.int32, s.shape, 0)
        s = jnp.where(own, s, -jnp.inf)
        mn = jnp.maximum(m, jnp.max(s, axis=1, keepdims=True))
        alpha = jnp.exp(m - mn)
        pr = jnp.exp(s - mn)
        l = alpha * l + jnp.sum(pr, axis=1, keepdims=True)
        a = alpha * a + jnp.dot(pr.astype(BF16), v_ref[...].astype(BF16), preferred_element_type=F32)
        m = mn
    m_s[...], l_s[...], acc[...] = m, l, a
    carry[...] = jnp.broadcast_to(later_pages, carry.shape)

    @pl.when(p == pl.num_programs(1) - 1)
    def _():
        o_ref[...] = a / l


def fox_decode(page_table, layer, q, k_new, v_new, fq, cache_k, cache_v, later_logs, page_totals, scale):
    nb, n_pages = page_table.shape
    depth, n_pool, page, n_heads, dh = cache_k.shape
    rows = page * n_heads
    ck = cache_k.reshape(depth, n_pool, rows, dh)
    cv = cache_v.reshape(depth, n_pool, rows, dh)
    pps = _largest_divisor(n_pages, 4)
    kern = functools.partial(_fox_decode_kernel, n_heads=n_heads, scale=scale, pages_per_step=pps)
    per_b = lambda w: pl.BlockSpec((None, n_heads, w), lambda i, p, pt: (i, 0, 0))
    paged = lambda a, u: pl.BlockSpec((None, None) + a.shape[2:],
                                      lambda i, p, pt: (layer, pt[i, n_pages - 1 - (p * pps + u)], 0, 0))
    page_specs, page_args = [], []
    for u in range(pps):
        for a in (ck, cv, later_logs, page_totals):
            page_specs.append(paged(a, u))
            page_args.append(a)
    grid_spec = pltpu.PrefetchScalarGridSpec(
        num_scalar_prefetch=1, grid=(nb, n_pages // pps),
        in_specs=[per_b(dh), per_b(dh), per_b(dh), per_b(1)] + page_specs,
        out_specs=per_b(dh),
        scratch_shapes=[pltpu.VMEM((n_heads, 1), F32), pltpu.VMEM((n_heads, 1), F32), pltpu.VMEM((n_heads, dh), F32),
                        pltpu.VMEM((8, LANES), F32)])
    return pl.pallas_call(
        kern, grid_spec=grid_spec, out_shape=jax.ShapeDtypeStruct((nb, n_heads, dh), F32),
        compiler_params=_cparams(("arbitrary", "arbitrary")), name="fox_decode")(
            page_table, q, k_new, v_new, fq, *page_args)


def kernel(x_prompt, x_sample, cache_k, cache_v, cache_lf, state_c, state_n, state_m, page_table, meta, norm1_g,
           w_in, b_in, q_norm_g, k_norm_g, mh_norm_g, p_a, p_b, w_out, norm2_g, w_gate, w_up, w_down):
    nb, seq, d = x_prompt.shape
    db = x_sample.shape[0]
    assert x_sample.shape[1] == 1
    n_meta = meta.shape[0]
    depth = w_in.shape[0]
    _, _, hm, dqk, dv = state_c.shape
    _, _, _, hf, dh = cache_k.shape
    d_ff = w_gate.shape[2]
    assert dh == LANES and dqk == LANES and seq % CHUNK == 0 and n_meta + db <= SMALL_ROWS
    assert 2 * hm + hf <= LANES
    r = nb * seq
    sizes = (hm * dqk, hm * dqk, hm * dv, hm * dv, hm, hm, hf * dh, hf * dh, hf * dh, hf, d, d)
    st = [0]
    for sz in sizes:
        st.append(st[-1] + sz)
    n_gate = 2 * hm + hf
    n_mqkv, n_mo, n_f = st[3], hm * dv, hf * dh
    d_ffp = d_ff + (-d_ff % 1024)
    scale_f = dh ** -0.5
    sample = slice(n_meta, n_meta + db)

    w_m = cast_weights(w_in, 0, st[4], d, st[4], "cast_w_m")
    w_f = cast_weights(w_in, st[6], 3 * n_f, d, 3 * n_f, "cast_w_f")
    w_g = cast_weights(w_in, st[10], 2 * d, d, 2 * d, "cast_w_g")
    w_small = jnp.concatenate([w_in[:, :, st[4]:st[6]], w_in[:, :, st[9]:st[10]],
                               jnp.zeros((depth, d, LANES - n_gate), F32)], axis=2).astype(BF16)
    pa = cast_weights(p_a, 0, d, p_a.shape[1], d, "cast_p_a")
    pb = cast_weights(p_b, 0, d, p_b.shape[1], d, "cast_p_b")
    wo = cast_weights(w_out, 0, d, d, d, "cast_w_out")
    wg = cast_weights(w_gate, 0, d_ff, d, d_ffp, "cast_w_gate")
    wu = cast_weights(w_up, 0, d_ff, d, d_ffp, "cast_w_up")
    wd = cast_weights(w_down, 0, d, d_ffp, d, "cast_w_down")

    xb = x_prompt.reshape(r, d)
    xs = jnp.concatenate([meta.astype(F32), x_sample.reshape(db, d),
                          jnp.zeros((SMALL_ROWS - n_meta - db, d), F32)], axis=0)
    qk_scale = jnp.concatenate([jnp.ones((1, hm * dqk), F32), jnp.full((1, hm * dqk), dqk ** -0.5, F32),
                                jnp.ones((1, hm * dv), F32)], axis=1)
    zero_row = jnp.zeros((1, LANES), F32)
    later_logs, page_totals = page_logs(cache_lf)
    outs = [[] for _ in range(12)]

    for l in range(depth):
        b_l = b_in[l].reshape(1, -1)
        b_small = jnp.concatenate([b_l[:, st[4]:st[6]], b_l[:, st[9]:st[10]], jnp.zeros((1, LANES - n_gate), F32)], axis=1)

        hb, hs = rmsnorm_rows(xb, norm1_g[l]), rmsnorm_rows(xs, norm1_g[l])
        lhs = [(hb, hs)]
        qkv, qkv_s = fused_matmul(lhs, l, [(w_m, 0, 0, b_l[:, :n_mqkv])], [], [qk_scale], _epi_scale, n_mqkv, BF16,
                                  1024, 1024, "in_qkv")
        og, og_s = fused_matmul(lhs, l, [(w_m, 0, n_mqkv, b_l[:, st[3]:st[4]])], [], [], _epi_identity, n_mo, BF16,
                                1024, 1024, "in_mo")
        fq, fq_s = fused_matmul(lhs, l, [(w_f, 0, 0, b_l[:, st[6]:st[7]])], [], [q_norm_g[l].reshape(1, dh)],
                                _epi_headnorm, n_f, BF16, 1024, 1024, "in_fq")
        fk, fk_s = fused_matmul(lhs, l, [(w_f, 0, n_f, b_l[:, st[7]:st[8]])], [], [k_norm_g[l].reshape(1, dh)],
                                _epi_headnorm, n_f, F32, 1024, 1024, "in_fk")
        fv, fv_s = fused_matmul(lhs, l, [(w_f, 0, 2 * n_f, b_l[:, st[8]:st[9]])], [], [], _epi_identity, n_f, F32,
                                1024, 1024, "in_fv")
        gab, gab_s = fused_matmul(lhs, l, [(w_g, 0, 0, b_l[:, st[10]:])], [], [], _epi_identity, 2 * d, BF16,
                                  1024, 1024, "in_gab")
        ls, cbs, _, lst, cbst, _ = gate_logs(hs, w_small, l, b_small, zero_row, 1, hm, n_gate)
        lb, cbb, cfb, lbt, cbbt, cfbt = gate_logs(hb, w_small, l, b_small, cbs[n_meta - 1:n_meta], nb, hm, n_gate)

        mh_g = mh_norm_g[l].reshape(1, hm * dv)
        hm_meta, c_m, n_m, m_m = mlstm_chunks(
            qkv_s, og_s, mh_g, ls, cbs, lst, cbst, jnp.zeros((hm, dqk, dv), F32), jnp.zeros((hm, 1, dqk), F32),
            jnp.zeros((hm, 1, LANES), F32), 1, n_meta)
        hm_b, c_p, n_p, m_p = mlstm_chunks(qkv, og, mh_g, lb, cbb, lbt, cbbt, c_m[0], n_m[0], m_m[0], nb, CHUNK)
        hm_d, c_d, n_d, m_d = mlstm_step(
            qkv_s[sample, :hm * dqk].reshape(db, hm, dqk), qkv_s[sample, hm * dqk:2 * hm * dqk].reshape(db, hm, dqk),
            qkv_s[sample, 2 * hm * dqk:].reshape(db, hm, dv), og_s[sample].reshape(db, hm, dv),
            mh_g.reshape(hm, dv), ls[sample, :hm].reshape(db, hm, 1), ls[sample, hm:2 * hm].reshape(db, hm, 1),
            state_c[l], state_n[l].reshape(db, hm, 1, dqk), state_m[l].reshape(db, hm, 1))
        pad_rows = jnp.zeros((SMALL_ROWS - n_meta - db, hm * dv), BF16)
        hm_s = jnp.concatenate([hm_meta[:n_meta], hm_d.reshape(db, hm * dv).astype(BF16), pad_rows], axis=0)

        hf_meta = fox_prompt(fq_s, fk_s, fv_s, fk_s, fv_s, cbs, cbst, cbst, 1, n_meta, 2 * hm, scale_f, True)
        hf_b = fox_prompt(fq, fk, fv, fk_s, fv_s, cfb, cfbt, cbst, nb, n_meta, 2 * hm, scale_f, False)
        hf_d = fox_decode(page_table, l, fq_s[sample].reshape(db, hf, dh), fk_s[sample].reshape(db, hf, dh),
                          fv_s[sample].reshape(db, hf, dh), ls[sample, 2 * hm:n_gate].reshape(db, hf, 1),
                          cache_k, cache_v, later_logs, page_totals, scale_f)
        hf_s = jnp.concatenate([hf_meta[:n_meta], hf_d.reshape(db, hf * dh).astype(BF16), pad_rows], axis=0)

        u, u_s = fused_matmul([(hm_b, hm_s), (hf_b, hf_s)], l, [(pa, 0, 0, None), (pb, 1, 0, None)],
                              [(gab, gab_s, 0), (gab, gab_s, d)], [], _epi_gated2, d, BF16, 512, 1024, "merge_gate")
        x1, x1_s = fused_matmul([(u, u_s)], l, [(wo, 0, 0, None)], [(xb, xs, 0)], [], _epi_residual, d, F32,
                                512, 1024, "out_proj")
        h2, h2_s = rmsnorm_rows(x1, norm2_g[l]), rmsnorm_rows(x1_s, norm2_g[l])
        act, act_s = fused_matmul([(h2, h2_s)], l, [(wg, 0, 0, None), (wu, 0, 0, None)], [], [], _epi_swiglu,
                                  d_ffp, BF16, 512, 1024, "ffn_up")
        xb, xs = fused_matmul([(act, act_s)], l, [(wd, 0, 0, None)], [(x1, x1_s, 0)], [], _epi_residual, d, F32,
                              256, 1024, "ffn_down", single_buffer_w=True)

        with_meta = lambda small, big, tail: jnp.concatenate(
            [jnp.broadcast_to(small[None, :n_meta], (nb, n_meta) + tail), big.reshape((nb, seq) + tail)],
            axis=1)
        outs[0].append(with_meta(fk_s.reshape(SMALL_ROWS, hf, dh), fk, (hf, dh)))
        outs[1].append(with_meta(fv_s.reshape(SMALL_ROWS, hf, dh), fv, (hf, dh)))
        outs[2].append(with_meta(ls[:, 2 * hm:n_gate], lb[:, 2 * hm:n_gate], (hf,)))
        outs[3].append(c_p)
        outs[4].append(n_p.reshape(nb, hm, dqk))
        outs[5].append(m_p[:, :, 0, 0])
        outs[6].append(fk_s[sample].reshape(db, 1, hf, dh))
        outs[7].append(fv_s[sample].reshape(db, 1, hf, dh))
        outs[8].append(ls[sample, 2 * hm:n_gate].reshape(db, 1, hf))
        outs[9].append(c_d)
        outs[10].append(n_d.reshape(db, hm, dqk))
        outs[11].append(m_d.reshape(db, hm))

    return (xb.reshape(nb, seq, d), xs[sample].reshape(db, 1, d)) + tuple(jnp.stack(o) for o in outs)
```

```python
import functools

import jax
import jax.numpy as jnp
from jax import lax
from jax.experimental import pallas as pl
from jax.experimental.pallas import tpu as pltpu

F32 = jnp.float32
BF16 = jnp.bfloat16
EPS = 1e-6
LANES = 128
CHUNK = 128
SMALL_ROWS = 128
VMEM_LIMIT = 52 * 1024 * 1024
NT_DIMS = (((1,), (1,)), ((), ()))


def _cparams(sem):
    return pltpu.CompilerParams(dimension_semantics=sem, vmem_limit_bytes=VMEM_LIMIT)


def _largest_divisor(n, cap, multiple_of=1):
    return max(k for k in range(multiple_of, cap + 1, multiple_of) if n % k == 0)


def _cast_kernel(*refs, shift, valid_rows, valid_cols):
    if shift:
        a_ref, b_ref, o_ref = refs
        wide = jnp.concatenate([a_ref[...], b_ref[...]], axis=1)
        x = pltpu.roll(wide, wide.shape[1] - shift, 1)[:, :a_ref.shape[1]]
    else:
        a_ref, o_ref = refs
        x = a_ref[...]
    tr, tc = x.shape
    if valid_rows is not None:
        rows = pl.program_id(1) * tr + lax.broadcasted_iota(jnp.int32, x.shape, 0)
        x = jnp.where(rows < valid_rows, x, 0.0)
    if valid_cols is not None:
        cols = pl.program_id(2) * tc + lax.broadcasted_iota(jnp.int32, x.shape, 1)
        x = jnp.where(cols < valid_cols, x, 0.0)
    o_ref[...] = x.astype(o_ref.dtype)


def cast_weights(w, col_start, n_cols, rows_out, cols_out, name):
    depth, rows, cols = w.shape
    shift = col_start % LANES
    base = col_start - shift
    tr = _largest_divisor(rows_out, 1024, 8)
    tc = _largest_divisor(cols_out, 2816, LANES)
    assert base % tc == 0 and (not shift or cols_out == n_cols)
    in_specs = [pl.BlockSpec((None, tr, tc), lambda l, i, j, o=base // tc: (l, i, j + o))]
    args = [w]
    if shift:
        in_specs.append(pl.BlockSpec((None, tr, LANES), lambda l, i, j, o=base // LANES, s=tc // LANES: (l, i, o + (j + 1) * s)))
        args.append(w)
    kern = functools.partial(_cast_kernel, shift=shift, valid_rows=rows if rows_out != rows else None,
                             valid_cols=n_cols if cols_out != n_cols else None)
    return pl.pallas_call(
        kern, grid=(depth, rows_out // tr, cols_out // tc), in_specs=in_specs,
        out_specs=pl.BlockSpec((None, tr, tc), lambda l, i, j: (l, i, j)),
        out_shape=jax.ShapeDtypeStruct((depth, rows_out, cols_out), BF16),
        compiler_params=_cparams(("parallel", "parallel", "parallel")), name=name)(*args)


def _cast_t_kernel(*refs, shift):
    if shift:
        a_ref, b_ref, o_ref = refs
        x = jnp.concatenate([a_ref[shift:, :], b_ref[...]], axis=0)
    else:
        a_ref, o_ref = refs
        x = a_ref[...]
    o_ref[...] = x.T.astype(o_ref.dtype)


def cast_weights_t(wt, row_start, n_rows, name):
    depth, _, k = wt.shape
    tk = _largest_divisor(k, 2048, LANES)
    tn = _largest_divisor(n_rows, 1024, LANES)
    shift = row_start % tn
    base = row_start - shift
    assert shift % 8 == 0 and (not shift or tn % shift == 0)
    in_specs = [pl.BlockSpec((None, tn, tk), lambda l, i, j, o=base // tn: (l, j + o, i))]
    args = [wt]
    if shift:
        in_specs.append(pl.BlockSpec((None, shift, tk),
                                     lambda l, i, j, o=base // shift, s=tn // shift: (l, o + (j + 1) * s, i)))
        args.append(wt)
    return pl.pallas_call(
        functools.partial(_cast_t_kernel, shift=shift), grid=(depth, k // tk, n_rows // tn), in_specs=in_specs,
        out_specs=pl.BlockSpec((None, tk, tn), lambda l, i, j: (l, i, j)),
        out_shape=jax.ShapeDtypeStruct((depth, k, n_rows), BF16),
        compiler_params=_cparams(("parallel", "parallel", "parallel")), name=name)(*args)


def _gate_rows_kernel(a_ref, b_ref, o_ref):
    pad = jnp.zeros((LANES - a_ref.shape[0] - b_ref.shape[0], a_ref.shape[1]), F32)
    o_ref[...] = jnp.concatenate([a_ref[...], b_ref[...], pad], axis=0).T.astype(o_ref.dtype)


def gate_weights_t(wt, start_a, n_a, start_b, n_b):
    depth, _, k = wt.shape
    assert n_a % 8 == 0 and n_b % 8 == 0 and start_a % n_a == 0 and start_b % n_b == 0
    tk = _largest_divisor(k, 2048, LANES)
    return pl.pallas_call(
        _gate_rows_kernel, grid=(depth, k // tk),
        in_specs=[pl.BlockSpec((None, n_a, tk), lambda l, i, o=start_a // n_a: (l, o, i)),
                  pl.BlockSpec((None, n_b, tk), lambda l, i, o=start_b // n_b: (l, o, i))],
        out_specs=pl.BlockSpec((None, tk, LANES), lambda l, i: (l, i, 0)),
        out_shape=jax.ShapeDtypeStruct((depth, k, LANES), BF16),
        compiler_params=_cparams(("parallel", "parallel")), name="cast_w_small")(wt, wt)


def _norm_kernel(x_ref, g_ref, o_ref):
    x = x_ref[...]
    y = x * lax.rsqrt(jnp.mean(x * x, axis=-1, keepdims=True) + EPS)
    o_ref[...] = (y * g_ref[...]).astype(o_ref.dtype)


def rmsnorm_rows(x, g):
    m, d = x.shape
    tm = min(256, m)
    return pl.pallas_call(
        _norm_kernel, grid=(m // tm,),
        in_specs=[pl.BlockSpec((tm, d), lambda i: (i, 0)), pl.BlockSpec((1, d), lambda i: (0, 0))],
        out_specs=pl.BlockSpec((tm, d), lambda i: (i, 0)),
        out_shape=jax.ShapeDtypeStruct((m, d), BF16),
        compiler_params=_cparams(("parallel",)), name="rmsnorm")(x, g.reshape(1, d))


def _mm_kernel(*refs, n_lhs, ws, n_extra, n_const, epi):
    it = iter(refs)
    lhs_big = [next(it) for _ in range(n_lhs)]
    lhs_small = [next(it) for _ in range(n_lhs)]
    w_refs = [next(it) for _ in ws]
    b_refs = [next(it) if has_bias else None for (_, has_bias) in ws]
    ex_big = [next(it) for _ in range(n_extra)]
    ex_small = [next(it) for _ in range(n_extra)]
    consts = [next(it) for _ in range(n_const)]
    out_big = next(it)
    out_small = next(it)

    def run(lhs, extras, out):
        accs = []
        for k, (li, _) in enumerate(ws):
            a = jnp.dot(lhs[li][...], w_refs[k][...], preferred_element_type=F32)
            if b_refs[k] is not None:
                a = a + b_refs[k][...]
            accs.append(a)
        out[...] = epi(accs, [e[...] for e in extras], [c[...] for c in consts]).astype(out.dtype)

    run(lhs_big, ex_big, out_big)

    @pl.when(pl.program_id(1) == 0)
    def _():
        run(lhs_small, ex_small, out_small)


def fused_matmul(lhs, layer, weights, extras, consts, epi, n_cols, out_dtype, tm, tn, name, single_buffer_w=False):
    r = lhs[0][0].shape[0]
    s = lhs[0][1].shape[0]
    tm = min(tm, r)
    assert r % tm == 0 and n_cols % tn == 0
    in_specs, args = [], []
    for big, _ in lhs:
        in_specs.append(pl.BlockSpec((tm, big.shape[1]), lambda j, i: (i, 0)))
        args.append(big)
    for _, small in lhs:
        in_specs.append(pl.BlockSpec(small.shape, lambda j, i: (0, 0)))
        args.append(small)
    w_mode = dict(pipeline_mode=pl.Buffered(1)) if single_buffer_w else {}
    for w, _, off, _ in weights:
        assert off % tn == 0
        in_specs.append(pl.BlockSpec((None, w.shape[1], tn), lambda j, i, o=off // tn: (layer, 0, j + o), **w_mode))
        args.append(w)
    for _, _, _, b in weights:
        if b is not None:
            assert b.shape == (1, n_cols)
            in_specs.append(pl.BlockSpec((1, tn), lambda j, i: (0, j)))
            args.append(b)
    for big, _, off in extras:
        assert off % tn == 0
        in_specs.append(pl.BlockSpec((tm, tn), lambda j, i, o=off // tn: (i, j + o)))
        args.append(big)
    for _, small, off in extras:
        in_specs.append(pl.BlockSpec((s, tn), lambda j, i, o=off // tn: (0, j + o)))
        args.append(small)
    for c in consts:
        if c.shape == (1, n_cols):
            in_specs.append(pl.BlockSpec((1, tn), lambda j, i: (0, j)))
        else:
            in_specs.append(pl.BlockSpec(c.shape, lambda j, i, nd=c.ndim: (0,) * nd))
        args.append(c)
    kern = functools.partial(
        _mm_kernel, n_lhs=len(lhs), ws=[(li, b is not None) for _, li, _, b in weights],
        n_extra=len(extras), n_const=len(consts), epi=epi)
    return pl.pallas_call(
        kern, grid=(n_cols // tn, r // tm), in_specs=in_specs,
        out_specs=[pl.BlockSpec((tm, tn), lambda j, i: (i, j)), pl.BlockSpec((s, tn), lambda j, i: (0, j))],
        out_shape=[jax.ShapeDtypeStruct((r, n_cols), out_dtype), jax.ShapeDtypeStruct((s, n_cols), out_dtype)],
        compiler_params=_cparams(("parallel", "arbitrary")), name=name)(*args)


def _epi_scale(accs, extras, consts):
    return accs[0] * consts[0]


def _epi_identity(accs, extras, consts):
    return accs[0]


def _epi_headnorm(accs, extras, consts):
    y, g = accs[0], consts[0]
    outs = []
    for h in range(y.shape[1] // LANES):
        yh = y[:, h * LANES:(h + 1) * LANES]
        outs.append(yh * lax.rsqrt(jnp.mean(yh * yh, axis=-1, keepdims=True) + EPS) * g)
    return jnp.concatenate(outs, axis=-1)


def _epi_gated2(accs, extras, consts):
    return (jax.nn.sigmoid(extras[0].astype(F32)) * accs[0] + jax.nn.sigmoid(extras[1].astype(F32)) * accs[1])


def _epi_residual(accs, extras, consts):
    return extras[0] + accs[0]


def _epi_swiglu(accs, extras, consts):
    return jax.nn.silu(accs[0]) * accs[1]


def _gates_kernel(h_ref, w_ref, b_ref, c0_ref, l_ref, cb_ref, cf_ref, lt_ref, cbt_ref, cft_ref, carry,
                  *, n_lin, n_gate, cs):
    @pl.when(pl.program_id(1) == 0)
    def _():
        carry[...] = c0_ref[...]

    z = jnp.dot(h_ref[...], w_ref[...], preferred_element_type=F32) + b_ref[...]
    lane = lax.broadcasted_iota(jnp.int32, z.shape, 1)
    logs = jnp.where(lane < n_lin, z, jnp.where(lane < n_gate, jax.nn.log_sigmoid(z), 0.0))
    tri = (lax.broadcasted_iota(jnp.int32, (cs, cs), 1) <= lax.broadcasted_iota(jnp.int32, (cs, cs), 0)).astype(F32)
    for s in range(z.shape[0] // cs):
        rows = slice(s * cs, (s + 1) * cs)
        lc = logs[rows]
        cb = jnp.dot(tri, lc, precision=lax.Precision.HIGHEST, preferred_element_type=F32)
        cf = cb + carry[...]
        carry[...] = cf[cs - 1:cs, :]
        l_ref[rows, :] = lc
        cb_ref[rows, :] = cb
        cf_ref[rows, :] = cf
        lt_ref[:, rows] = lc.T
        cbt_ref[:, rows] = cb.T
        cft_ref[:, rows] = cf.T


def gate_logs(h, w, layer, bias, carry0, n_seq, n_lin, n_gate):
    r, d = h.shape
    seq = r // n_seq
    tile = min(512, seq)
    nt = seq // tile
    cs = min(CHUNK, tile)
    row = pl.BlockSpec((tile, LANES), lambda b, t: (b * nt + t, 0))
    col = pl.BlockSpec((LANES, tile), lambda b, t: (0, b * nt + t))
    kern = functools.partial(_gates_kernel, n_lin=n_lin, n_gate=n_gate, cs=cs)
    return pl.pallas_call(
        kern, grid=(n_seq, nt),
        in_specs=[pl.BlockSpec((tile, d), lambda b, t: (b * nt + t, 0)),
                  pl.BlockSpec((None, d, LANES), lambda b, t: (layer, 0, 0)),
                  pl.BlockSpec((1, LANES), lambda b, t: (0, 0)),
                  pl.BlockSpec((1, LANES), lambda b, t: (0, 0))],
        out_specs=[row, row, row, col, col, col],
        out_shape=[jax.ShapeDtypeStruct((r, LANES), F32)] * 3 + [jax.ShapeDtypeStruct((LANES, r), F32)] * 3,
        scratch_shapes=[pltpu.VMEM((1, LANES), F32)],
        compiler_params=_cparams(("arbitrary", "arbitrary")), name="gate_logs")(h, w, bias, carry0)


def _mlstm_head(q, k, v, bcol, brow, icol, irow, c0, n0, m0, nv):
    ln = q.shape[0]
    g = bcol + m0
    d = bcol - brow + irow
    ti = lax.broadcasted_iota(jnp.int32, (ln, ln), 0)
    si = lax.broadcasted_iota(jnp.int32, (ln, ln), 1)
    d = jnp.where(si <= ti, d, -jnp.inf)
    m = jnp.maximum(g, jnp.max(d, axis=1, keepdims=True))
    w = jnp.exp(d - m)
    inter = jnp.exp(g - m)
    s = lax.dot_general(q, k, NT_DIMS, preferred_element_type=F32) * w
    num = (jnp.dot(s.astype(BF16), v, preferred_element_type=F32)
           + inter * jnp.dot(q, c0.astype(BF16), preferred_element_type=F32))
    den = jnp.sum(s, axis=1, keepdims=True) + inter * jnp.sum(q.astype(F32) * n0, axis=1, keepdims=True)
    h = num / jnp.maximum(jnp.abs(den), jnp.exp(-m))
    m_end = m[nv - 1:nv]
    sidx = lax.broadcasted_iota(jnp.int32, (ln, 1), 0)
    w_end = jnp.where(sidx < nv, jnp.exp(bcol[nv - 1:nv] - bcol + icol - m_end), 0.0)
    decay = jnp.exp(g[nv - 1:nv] - m_end)
    kw = k.astype(F32) * w_end
    c1 = decay * c0 + jnp.dot(kw.T.astype(BF16), v, preferred_element_type=F32)
    n1 = decay * n0 + jnp.sum(kw, axis=0, keepdims=True)
    return h, c1, n1, m_end


def _mlstm_kernel(q_ref, k_ref, v_ref, og_ref, g_ref, l_ref, cb_ref, lt_ref, cbt_ref, ci_ref, ni_ref, mi_ref,
                  hm_ref, co_ref, no_ref, mo_ref, c_s, n_s, m_s, *, n_heads, dqk, dv, nv):
    c = pl.program_id(1)

    @pl.when(c == 0)
    def _():
        c_s[...] = ci_ref[...]
        n_s[...] = ni_ref[...]
        m_s[...] = mi_ref[...]

    logs, cb, logs_t, cb_t = l_ref[...], cb_ref[...], lt_ref[...], cbt_ref[...]
    for h in range(n_heads):
        hh, c1, n1, m_end = _mlstm_head(
            q_ref[:, h * dqk:(h + 1) * dqk], k_ref[:, h * dqk:(h + 1) * dqk], v_ref[:, h * dv:(h + 1) * dv],
            cb[:, n_heads + h:n_heads + h + 1], cb_t[n_heads + h:n_heads + h + 1, :],
            logs[:, h:h + 1], logs_t[h:h + 1, :], c_s[h], n_s[h], m_s[h][:, 0:1], nv)
        c_s[h] = c1
        n_s[h] = n1
        m_s[h] = jnp.broadcast_to(m_end, (1, LANES))
        cols = slice(h * dv, (h + 1) * dv)
        hn = hh * lax.rsqrt(jnp.mean(hh * hh, axis=-1, keepdims=True) + EPS) * g_ref[:, cols]
        hm_ref[:, cols] = (hn * jax.nn.sigmoid(og_ref[:, cols].astype(F32))).astype(hm_ref.dtype)

    @pl.when(c == pl.num_programs(1) - 1)
    def _():
        co_ref[0] = c_s[...]
        no_ref[0] = n_s[...]
        mo_ref[0] = m_s[...]


def mlstm_chunks(qkv, og, mh_g, logs, cb, logs_t, cb_t, c_init, n_init, m_init, n_seq, nv):
    r = qkv.shape[0]
    n_heads, dqk, dv = c_init.shape
    nc = r // n_seq // CHUNK
    hq, hv = n_heads * dqk, n_heads * dv
    assert hv == 2 * hq
    rows = lambda b, c: b * nc + c
    kern = functools.partial(_mlstm_kernel, n_heads=n_heads, dqk=dqk, dv=dv, nv=nv)
    full3 = lambda shape: pl.BlockSpec(shape, lambda b, c: (0, 0, 0))
    return pl.pallas_call(
        kern, grid=(n_seq, nc),
        in_specs=[pl.BlockSpec((CHUNK, hq), lambda b, c: (rows(b, c), 0)),
                  pl.BlockSpec((CHUNK, hq), lambda b, c: (rows(b, c), 1)),
                  pl.BlockSpec((CHUNK, hv), lambda b, c: (rows(b, c), 1)),
                  pl.BlockSpec((CHUNK, hv), lambda b, c: (rows(b, c), 0)),
                  pl.BlockSpec((1, hv), lambda b, c: (0, 0)),
                  pl.BlockSpec((CHUNK, LANES), lambda b, c: (rows(b, c), 0)),
                  pl.BlockSpec((CHUNK, LANES), lambda b, c: (rows(b, c), 0)),
                  pl.BlockSpec((LANES, CHUNK), lambda b, c: (0, rows(b, c))),
                  pl.BlockSpec((LANES, CHUNK), lambda b, c: (0, rows(b, c))),
                  full3(c_init.shape), full3(n_init.shape), full3(m_init.shape)],
        out_specs=[pl.BlockSpec((CHUNK, hv), lambda b, c: (rows(b, c), 0)),
                   pl.BlockSpec((1,) + c_init.shape, lambda b, c: (b, 0, 0, 0)),
                   pl.BlockSpec((1,) + n_init.shape, lambda b, c: (b, 0, 0, 0)),
                   pl.BlockSpec((1,) + m_init.shape, lambda b, c: (b, 0, 0, 0))],
        out_shape=[jax.ShapeDtypeStruct((r, hv), BF16),
                   jax.ShapeDtypeStruct((n_seq,) + c_init.shape, F32),
                   jax.ShapeDtypeStruct((n_seq,) + n_init.shape, F32),
                   jax.ShapeDtypeStruct((n_seq,) + m_init.shape, F32)],
        scratch_shapes=[pltpu.VMEM(c_init.shape, F32), pltpu.VMEM(n_init.shape, F32), pltpu.VMEM(m_init.shape, F32)],
        compiler_params=_cparams(("arbitrary", "arbitrary")), name="mlstm_chunks")(
            qkv, qkv, qkv, og, mh_g, logs, cb, logs_t, cb_t, c_init, n_init, m_init)


def _mlstm_step_kernel(q_ref, k_ref, v_ref, og_ref, g_ref, li_ref, lf_ref, c_ref, n_ref, m_ref,
                       hm_ref, co_ref, no_ref, mo_ref, *, n_heads):
    q, k = q_ref[0].astype(F32), k_ref[0].astype(F32)
    v = v_ref[0].astype(F32)
    og = og_ref[0].astype(F32)
    pad = jnp.zeros((LANES - n_heads, q.shape[1]), F32)
    q_t = jnp.concatenate([q, pad], axis=0).T
    k_t = jnp.concatenate([k, pad], axis=0).T
    logi, logf, m0 = li_ref[0], lf_ref[0], m_ref[0]
    g = logf + m0
    m = jnp.maximum(g, logi)
    w = jnp.exp(logi - m)
    inter = jnp.exp(g - m)
    floor = jnp.exp(-m)
    for h in range(n_heads):
        c0, n0 = c_ref[0, h], n_ref[0, h]
        qh, kh, vh = q[h:h + 1], k[h:h + 1], v[h:h + 1]
        wh, ih = w[h:h + 1], inter[h:h + 1]
        s = jnp.sum(qh * kh, axis=1, keepdims=True) * wh
        num = s * vh + ih * jnp.sum(q_t[:, h:h + 1] * c0, axis=0, keepdims=True)
        den = s + ih * jnp.sum(qh * n0, axis=1, keepdims=True)
        hh = num / jnp.maximum(jnp.abs(den), floor[h:h + 1])
        co_ref[0, h] = ih * c0 + (wh * k_t[:, h:h + 1]) * vh
        no_ref[0, h] = ih * n0 + wh * kh
        hn = hh * lax.rsqrt(jnp.mean(hh * hh, axis=-1, keepdims=True) + EPS) * g_ref[h:h + 1]
        hm_ref[0, h:h + 1] = hn * jax.nn.sigmoid(og[h:h + 1])
    mo_ref[0] = m


def mlstm_step(q, k, v, og, mh_g, logi, logf, c0, n0, m0):
    nb, n_heads, dqk = q.shape
    dv = v.shape[2]
    b3 = lambda a: pl.BlockSpec((1,) + a.shape[1:], lambda i: (i,) + (0,) * (a.ndim - 1))
    outs = [jax.ShapeDtypeStruct((nb, n_heads, dv), F32), jax.ShapeDtypeStruct(c0.shape, F32),
            jax.ShapeDtypeStruct(n0.shape, F32), jax.ShapeDtypeStruct(m0.shape, F32)]
    return pl.pallas_call(
        functools.partial(_mlstm_step_kernel, n_heads=n_heads), grid=(nb,),
        in_specs=[b3(q), b3(k), b3(v), b3(og), pl.BlockSpec(mh_g.shape, lambda i: (0, 0)),
                  b3(logi), b3(logf), b3(c0), b3(n0), b3(m0)],
        out_specs=[b3(o) for o in outs], out_shape=outs,
        compiler_params=_cparams(("parallel",)), name="mlstm_step")(q, k, v, og, mh_g, logi, logf, c0, n0, m0)


def _fox_kernel(q_ref, k_ref, v_ref, km_ref, vm_ref, cf_ref, cft_ref, cmt_ref, o_ref, kb, vb,
                *, tq, tk, n_meta, scale, lane0, causal_meta):
    h = pl.program_id(1)
    seq = q_ref.shape[0]
    if not causal_meta:
        kb[...] = k_ref[...].astype(BF16)
        vb[...] = v_ref[...].astype(BF16)
    km = km_ref[...].astype(BF16)
    vm = vm_ref[...].astype(BF16)
    lane = lax.broadcasted_iota(jnp.int32, (tq, LANES), 1)
    tri = lax.broadcasted_iota(jnp.int32, (tq, tk), 1) <= lax.broadcasted_iota(jnp.int32, (tq, tk), 0)

    def online(carry, s, vals):
        m, l, acc = carry
        mn = jnp.maximum(m, jnp.max(s, axis=1, keepdims=True))
        alpha = jnp.exp(m - mn)
        p = jnp.exp(s - mn)
        return (mn, alpha * l + jnp.sum(p, axis=1, keepdims=True),
                alpha * acc + jnp.dot(p.astype(BF16), vals, preferred_element_type=F32))

    for i in range(seq // tq):
        rows = slice(i * tq, (i + 1) * tq)
        q = q_ref[rows, :]
        fq = jnp.sum(jnp.where(lane == lane0 + h, cf_ref[rows, :], 0.0), axis=1, keepdims=True)
        s = lax.dot_general(q, km, NT_DIMS, preferred_element_type=F32) * scale + fq - cmt_ref[...]
        col = lax.broadcasted_iota(jnp.int32, s.shape, 1)
        ok = col < n_meta
        if causal_meta:
            ok = ok & (col <= lax.broadcasted_iota(jnp.int32, s.shape, 0))
        s = jnp.where(ok, s, -jnp.inf)
        m = jnp.max(s, axis=1, keepdims=True)
        p = jnp.exp(s - m)
        carry = (m, jnp.sum(p, axis=1, keepdims=True), jnp.dot(p.astype(BF16), vm, preferred_element_type=F32))
        if not causal_meta:
            for j in range(i + 1):
                cols = slice(j * tk, (j + 1) * tk)
                s = (lax.dot_general(q, kb[cols, :], NT_DIMS, preferred_element_type=F32) * scale
                     + fq - cft_ref[j:j + 1, :])
                if j == i:
                    s = jnp.where(tri, s, -jnp.inf)
                carry = online(carry, s, vb[cols, :])
        o_ref[rows, :] = (carry[2] / carry[1]).astype(o_ref.dtype)


def fox_prompt(q, k, v, k_meta, v_meta, cf, cft, cmt, n_seq, n_meta, lane0, scale, causal_meta):
    r, width = q.shape
    dh = LANES
    n_heads = width // dh
    seq = r // n_seq
    tq = tk = min(512, seq)
    s_rows = k_meta.shape[0]
    cft3 = cft.reshape(LANES, n_seq, seq // tk, tk)
    cmt3 = cmt.reshape(LANES, 1, s_rows)
    kern = functools.partial(_fox_kernel, tq=tq, tk=tk, n_meta=n_meta, scale=scale, lane0=lane0,
                             causal_meta=causal_meta)
    blk = pl.BlockSpec((seq, dh), lambda b, h: (b, h))
    return pl.pallas_call(
        kern, grid=(n_seq, n_heads),
        in_specs=[blk, blk, blk,
                  pl.BlockSpec((s_rows, dh), lambda b, h: (0, h)),
                  pl.BlockSpec((s_rows, dh), lambda b, h: (0, h)),
                  pl.BlockSpec((seq, LANES), lambda b, h: (b, 0)),
                  pl.BlockSpec((None, None, seq // tk, tk), lambda b, h: (lane0 + h, b, 0, 0)),
                  pl.BlockSpec((None, 1, s_rows), lambda b, h: (lane0 + h, 0, 0))],
        out_specs=blk,
        out_shape=jax.ShapeDtypeStruct((r, width), BF16),
        scratch_shapes=[pltpu.VMEM((seq, dh), BF16), pltpu.VMEM((seq, dh), BF16)],
        compiler_params=_cparams(("parallel", "arbitrary")), name="fox_prompt")(
            q, k, v, k_meta, v_meta, cf, cft3, cmt3)


def _page_logs_kernel(x_ref, r_ref, t_ref, *, n_heads, rpp, sub_pages):
    hi = lax.Precision.HIGHEST
    li = lax.broadcasted_iota(jnp.int32, (LANES, LANES), 0)
    lo = lax.broadcasted_iota(jnp.int32, (LANES, LANES), 1)
    same_head = (li % n_heads) == (lo % n_heads)
    w_row = (same_head & (li // n_heads >= lo // n_heads)).astype(F32)
    w_all = same_head.astype(F32)
    sb = sub_pages * rpp
    ri = lax.broadcasted_iota(jnp.int32, (sb, sb), 0)
    ro = lax.broadcasted_iota(jnp.int32, (sb, sb), 1)
    later_rows = ((ro > ri) & (ro // rpp == ri // rpp)).astype(F32)
    pi = lax.broadcasted_iota(jnp.int32, (sub_pages * 8, sb), 0)
    po = lax.broadcasted_iota(jnp.int32, (sub_pages * 8, sb), 1)
    page_rows = (po // rpp == pi // 8).astype(F32)
    for s in range(x_ref.shape[0] // sb):
        x = x_ref[s * sb:(s + 1) * sb, :]
        row_tot = jnp.dot(x, w_all, precision=hi, preferred_element_type=F32)
        r_ref[s * sb:(s + 1) * sb, :] = (jnp.dot(x, w_row, precision=hi, preferred_element_type=F32)
                                         + jnp.dot(later_rows, row_tot, precision=hi, preferred_element_type=F32) - x)
        t_ref[s * sub_pages * 8:(s + 1) * sub_pages * 8, :] = jnp.dot(page_rows, row_tot, precision=hi,
                                                                      preferred_element_type=F32)


def page_logs(cache_lf):
    depth, n_pool, page, n_heads = cache_lf.shape
    assert LANES % n_heads == 0 and (page * n_heads) % (8 * LANES) == 0
    rpp = page * n_heads // LANES
    n_pages = depth * n_pool
    sub_pages = _largest_divisor(n_pages, max(1, LANES // rpp))
    tile_pages = sub_pages * _largest_divisor(n_pages // sub_pages, 8)
    x = cache_lf.reshape(n_pages * rpp, LANES)
    kern = functools.partial(_page_logs_kernel, n_heads=n_heads, rpp=rpp, sub_pages=sub_pages)
    r, t = pl.pallas_call(
        kern, grid=(n_pages // tile_pages,),
        in_specs=[pl.BlockSpec((tile_pages * rpp, LANES), lambda i: (i, 0))],
        out_specs=[pl.BlockSpec((tile_pages * rpp, LANES), lambda i: (i, 0)),
                   pl.BlockSpec((tile_pages * 8, LANES), lambda i: (i, 0))],
        out_shape=[jax.ShapeDtypeStruct((n_pages * rpp, LANES), F32), jax.ShapeDtypeStruct((n_pages * 8, LANES), F32)],
        compiler_params=_cparams(("parallel",)), name="page_logs")(x)
    return r.reshape(depth, n_pool, rpp, LANES), t.reshape(depth, n_pool, 8, LANES)


def _fox_decode_kernel(pt_ref, q_ref, kn_ref, vn_ref, fq_ref, *refs, n_heads, scale, pages_per_step):
    page_refs = refs[:4 * pages_per_step]
    o_ref, m_s, l_s, acc, carry = refs[4 * pages_per_step:]
    p = pl.program_id(1)
    q = q_ref[...]

    @pl.when(p == 0)
    def _():
        m_s[...] = jnp.sum(q.astype(F32) * kn_ref[...], axis=1, keepdims=True) * scale
        l_s[...] = jnp.ones_like(l_s)
        acc[...] = vn_ref[...]
        carry[...] = jnp.zeros_like(carry)

    fq = fq_ref[...]
    later_pages = carry[0:1, :]
    m, l, a = m_s[...], l_s[...], acc[...]
    for u in range(pages_per_step):
        k_ref, v_ref, r_ref, t_ref = page_refs[4 * u:4 * u + 4]
        later = r_ref[...] + later_pages
        later_pages = later_pages + t_ref[0:1, :]
        bias = jnp.concatenate([jnp.broadcast_to(later[c:c + 1], (n_heads, LANES)) for c in range(later.shape[0])],
                               axis=1)
        s = lax.dot_general(q, k_ref[...].astype(BF16), NT_DIMS, preferred_element_type=F32) * scale + fq + bias
        own = (lax.broadcasted_iota(jnp.int32, s.shape, 1) % n_heads) == lax.broadcasted_iota(jnp.int32, s.shape, 0)
        s = jnp.where(own, s, -jnp.inf)
        mn = jnp.maximum(m, jnp.max(s, axis=1, keepdims=True))
        alpha = jnp.exp(m - mn)
        pr = jnp.exp(s - mn)
        l = alpha * l + jnp.sum(pr, axis=1, keepdims=True)
        a = alpha * a + jnp.dot(pr.astype(BF16), v_ref[...].astype(BF16), preferred_element_type=F32)
        m = mn
    m_s[...], l_s[...], acc[...] = m, l, a
    carry[...] = jnp.broadcast_to(later_pages, carry.shape)

    @pl.when(p == pl.num_programs(1) - 1)
    def _():
        o_ref[...] = a / l


def fox_decode(page_table, layer, q, k_new, v_new, fq, cache_k, cache_v, later_logs, page_totals, scale):
    nb, n_pages = page_table.shape
    depth, n_pool, page, n_heads, dh = cache_k.shape
    rows = page * n_heads
    ck = cache_k.reshape(depth, n_pool, rows, dh)
    cv = cache_v.reshape(depth, n_pool, rows, dh)
    pps = _largest_divisor(n_pages, 8)
    kern = functools.partial(_fox_decode_kernel, n_heads=n_heads, scale=scale, pages_per_step=pps)
    per_b = lambda w: pl.BlockSpec((None, n_heads, w), lambda i, p, pt: (i, 0, 0))
    paged = lambda a, u: pl.BlockSpec((None, None) + a.shape[2:],
                                      lambda i, p, pt: (layer, pt[i, n_pages - 1 - (p * pps + u)], 0, 0))
    page_specs, page_args = [], []
    for u in range(pps):
        for a in (ck, cv, later_logs, page_totals):
            page_specs.append(paged(a, u))
            page_args.append(a)
    grid_spec = pltpu.PrefetchScalarGridSpec(
        num_scalar_prefetch=1, grid=(nb, n_pages // pps),
        in_specs=[per_b(dh), per_b(dh), per_b(dh), per_b(1)] + page_specs,
        out_specs=per_b(dh),
        scratch_shapes=[pltpu.VMEM((n_heads, 1), F32), pltpu.VMEM((n_heads, 1), F32), pltpu.VMEM((n_heads, dh), F32),
                        pltpu.VMEM((8, LANES), F32)])
    return pl.pallas_call(
        kern, grid_spec=grid_spec, out_shape=jax.ShapeDtypeStruct((nb, n_heads, dh), F32),
        compiler_params=_cparams(("arbitrary", "arbitrary")), name="fox_decode")(
            page_table, q, k_new, v_new, fq, *page_args)


def kernel(x_prompt, x_sample, cache_k, cache_v, cache_lf, state_c, state_n, state_m, page_table, meta, norm1_g,
           w_in, b_in, q_norm_g, k_norm_g, mh_norm_g, p_a, p_b, w_out, norm2_g, w_gate, w_up, w_down):
    nb, seq, d = x_prompt.shape
    db = x_sample.shape[0]
    assert x_sample.shape[1] == 1
    n_meta = meta.shape[0]
    depth = w_in.shape[0]
    _, _, hm, dqk, dv = state_c.shape
    _, _, _, hf, dh = cache_k.shape
    d_ff = w_gate.shape[2]
    assert dh == LANES and dqk == LANES and seq % CHUNK == 0 and n_meta + db <= SMALL_ROWS
    assert 2 * hm + hf <= LANES
    r = nb * seq
    sizes = (hm * dqk, hm * dqk, hm * dv, hm * dv, hm, hm, hf * dh, hf * dh, hf * dh, hf, d, d)
    st = [0]
    for sz in sizes:
        st.append(st[-1] + sz)
    n_gate = 2 * hm + hf
    n_mqkv, n_mo, n_f = st[3], hm * dv, hf * dh
    d_ffp = d_ff + (-d_ff % 1024)
    scale_f = dh ** -0.5
    sample = slice(n_meta, n_meta + db)

    w_in_t = jnp.swapaxes(w_in, 1, 2)
    w_m = cast_weights_t(w_in_t, 0, st[4], "cast_w_m")
    w_f = cast_weights_t(w_in_t, st[6], 3 * n_f, "cast_w_f")
    w_g = cast_weights_t(w_in_t, st[10], 2 * d, "cast_w_g")
    w_small = gate_weights_t(w_in_t, st[4], 2 * hm, st[9], hf)
    pa = cast_weights(p_a, 0, d, p_a.shape[1], d, "cast_p_a")
    pb = cast_weights(p_b, 0, d, p_b.shape[1], d, "cast_p_b")
    wo = cast_weights(w_out, 0, d, d, d, "cast_w_out")
    wg = cast_weights(w_gate, 0, d_ff, d, d_ffp, "cast_w_gate")
    wu = cast_weights(w_up, 0, d_ff, d, d_ffp, "cast_w_up")
    wd = cast_weights(w_down, 0, d, d_ffp, d, "cast_w_down")

    xb = x_prompt.reshape(r, d)
    xs = jnp.concatenate([meta.astype(F32), x_sample.reshape(db, d),
                          jnp.zeros((SMALL_ROWS - n_meta - db, d), F32)], axis=0)
    qk_scale = jnp.concatenate([jnp.ones((1, hm * dqk), F32), jnp.full((1, hm * dqk), dqk ** -0.5, F32),
                                jnp.ones((1, hm * dv), F32)], axis=1)
    zero_row = jnp.zeros((1, LANES), F32)
    later_logs, page_totals = page_logs(cache_lf)
    outs = [[] for _ in range(12)]

    for l in range(depth):
        b_l = b_in[l].reshape(1, -1)
        b_small = jnp.concatenate([b_l[:, st[4]:st[6]], b_l[:, st[9]:st[10]], jnp.zeros((1, LANES - n_gate), F32)], axis=1)

        hb, hs = rmsnorm_rows(xb, norm1_g[l]), rmsnorm_rows(xs, norm1_g[l])
        lhs = [(hb, hs)]
        qkv, qkv_s = fused_matmul(lhs, l, [(w_m, 0, 0, b_l[:, :n_mqkv])], [], [qk_scale], _epi_scale, n_mqkv, BF16,
                                  1024, 1024, "in_qkv")
        og, og_s = fused_matmul(lhs, l, [(w_m, 0, n_mqkv, b_l[:, st[3]:st[4]])], [], [], _epi_identity, n_mo, BF16,
                                1024, 1024, "in_mo")
        fq, fq_s = fused_matmul(lhs, l, [(w_f, 0, 0, b_l[:, st[6]:st[7]])], [], [q_norm_g[l].reshape(1, dh)],
                                _epi_headnorm, n_f, BF16, 1024, 1024, "in_fq")
        fk, fk_s = fused_matmul(lhs, l, [(w_f, 0, n_f, b_l[:, st[7]:st[8]])], [], [k_norm_g[l].reshape(1, dh)],
                                _epi_headnorm, n_f, F32, 1024, 1024, "in_fk")
        fv, fv_s = fused_matmul(lhs, l, [(w_f, 0, 2 * n_f, b_l[:, st[8]:st[9]])], [], [], _epi_identity, n_f, F32,
                                1024, 1024, "in_fv")
        gab, gab_s = fused_matmul(lhs, l, [(w_g, 0, 0, b_l[:, st[10]:])], [], [], _epi_identity, 2 * d, BF16,
                                  1024, 1024, "in_gab")
        ls, cbs, _, lst, cbst, _ = gate_logs(hs, w_small, l, b_small, zero_row, 1, hm, n_gate)
        lb, cbb, cfb, lbt, cbbt, cfbt = gate_logs(hb, w_small, l, b_small, cbs[n_meta - 1:n_meta], nb, hm, n_gate)

        mh_g = mh_norm_g[l].reshape(1, hm * dv)
        hm_meta, c_m, n_m, m_m = mlstm_chunks(
            qkv_s, og_s, mh_g, ls, cbs, lst, cbst, jnp.zeros((hm, dqk, dv), F32), jnp.zeros((hm, 1, dqk), F32),
            jnp.zeros((hm, 1, LANES), F32), 1, n_meta)
        hm_b, c_p, n_p, m_p = mlstm_chunks(qkv, og, mh_g, lb, cbb, lbt, cbbt, c_m[0], n_m[0], m_m[0], nb, CHUNK)
        hm_d, c_d, n_d, m_d = mlstm_step(
            qkv_s[sample, :hm * dqk].reshape(db, hm, dqk), qkv_s[sample, hm * dqk:2 * hm * dqk].reshape(db, hm, dqk),
            qkv_s[sample, 2 * hm * dqk:].reshape(db, hm, dv), og_s[sample].reshape(db, hm, dv),
            mh_g.reshape(hm, dv), ls[sample, :hm].reshape(db, hm, 1), ls[sample, hm:2 * hm].reshape(db, hm, 1),
            state_c[l], state_n[l].reshape(db, hm, 1, dqk), state_m[l].reshape(db, hm, 1))
        pad_rows = jnp.zeros((SMALL_ROWS - n_meta - db, hm * dv), BF16)
        hm_s = jnp.concatenate([hm_meta[:n_meta], hm_d.reshape(db, hm * dv).astype(BF16), pad_rows], axis=0)

        hf_meta = fox_prompt(fq_s, fk_s, fv_s, fk_s, fv_s, cbs, cbst, cbst, 1, n_meta, 2 * hm, scale_f, True)
        hf_b = fox_prompt(fq, fk, fv, fk_s, fv_s, cfb, cfbt, cbst, nb, n_meta, 2 * hm, scale_f, False)
        hf_d = fox_decode(page_table, l, fq_s[sample].reshape(db, hf, dh), fk_s[sample].reshape(db, hf, dh),
                          fv_s[sample].reshape(db, hf, dh), ls[sample, 2 * hm:n_gate].reshape(db, hf, 1),
                          cache_k, cache_v, later_logs, page_totals, scale_f)
        hf_s = jnp.concatenate([hf_meta[:n_meta], hf_d.reshape(db, hf * dh).astype(BF16), pad_rows], axis=0)

        u, u_s = fused_matmul([(hm_b, hm_s), (hf_b, hf_s)], l, [(pa, 0, 0, None), (pb, 1, 0, None)],
                              [(gab, gab_s, 0), (gab, gab_s, d)], [], _epi_gated2, d, BF16, 512, 1024, "merge_gate")
        x1, x1_s = fused_matmul([(u, u_s)], l, [(wo, 0, 0, None)], [(xb, xs, 0)], [], _epi_residual, d, F32,
                                512, 1024, "out_proj")
        h2, h2_s = rmsnorm_rows(x1, norm2_g[l]), rmsnorm_rows(x1_s, norm2_g[l])
        act, act_s = fused_matmul([(h2, h2_s)], l, [(wg, 0, 0, None), (wu, 0, 0, None)], [], [], _epi_swiglu,
                                  d_ffp, BF16, 512, 1024, "ffn_up")
        xb, xs = fused_matmul([(act, act_s)], l, [(wd, 0, 0, None)], [(x1, x1_s, 0)], [], _epi_residual, d, F32,
                              256, 1024, "ffn_down", single_buffer_w=True)

        outs[0].append((fk_s.reshape(SMALL_ROWS, hf, dh), fk.reshape(r, hf, dh)))
        outs[1].append((fv_s.reshape(SMALL_ROWS, hf, dh), fv.reshape(r, hf, dh)))
        outs[2].append((ls[:, 2 * hm:n_gate], lb[:, 2 * hm:n_gate]))
        outs[3].append(c_p)
        outs[4].append(n_p.reshape(nb, hm, dqk))
        outs[5].append(m_p[:, :, 0, 0])
        outs[6].append(fk_s[sample].reshape(db, 1, hf, dh))
        outs[7].append(fv_s[sample].reshape(db, 1, hf, dh))
        outs[8].append(ls[sample, 2 * hm:n_gate].reshape(db, 1, hf))
        outs[9].append(c_d)
        outs[10].append(n_d.reshape(db, hm, dqk))
        outs[11].append(m_d.reshape(db, hm))

    def with_meta(pairs):
        pieces = [p for small, big in pairs for b in range(nb) for p in (small[:n_meta], big[b * seq:(b + 1) * seq])]
        return jnp.concatenate(pieces, axis=0).reshape((depth, nb, n_meta + seq) + pieces[0].shape[1:])

    return ((xb.reshape(nb, seq, d), xs[sample].reshape(db, 1, d)) + tuple(with_meta(o) for o in outs[:3])
            + tuple(jnp.stack(o) for o in outs[3:]))
```

```python
import functools

import jax
import jax.numpy as jnp
from jax import lax
from jax.experimental import pallas as pl
from jax.experimental.pallas import tpu as pltpu

F32 = jnp.float32
BF16 = jnp.bfloat16
EPS = 1e-6
LANES = 128
CHUNK = 128
SMALL_ROWS = 128
FOX_TQ, FOX_TK = 512, 512
VMEM_LIMIT = 58 * 1024 * 1024
NT_DIMS = (((1,), (1,)), ((), ()))


def _cparams(sem):
    return pltpu.CompilerParams(dimension_semantics=sem, vmem_limit_bytes=VMEM_LIMIT)


def _largest_divisor(n, cap, multiple_of=1):
    return max(k for k in range(multiple_of, cap + 1, multiple_of) if n % k == 0)


def _cast_kernel(*refs, shift, valid_rows, valid_cols):
    if shift:
        a_ref, b_ref, o_ref = refs
        wide = jnp.concatenate([a_ref[...], b_ref[...]], axis=1)
        x = pltpu.roll(wide, wide.shape[1] - shift, 1)[:, :a_ref.shape[1]]
    else:
        a_ref, o_ref = refs
        x = a_ref[...]
    tr, tc = x.shape
    if valid_rows is not None:
        rows = pl.program_id(1) * tr + lax.broadcasted_iota(jnp.int32, x.shape, 0)
        x = jnp.where(rows < valid_rows, x, 0.0)
    if valid_cols is not None:
        cols = pl.program_id(2) * tc + lax.broadcasted_iota(jnp.int32, x.shape, 1)
        x = jnp.where(cols < valid_cols, x, 0.0)
    o_ref[...] = x.astype(o_ref.dtype)


def cast_weights(w, col_start, n_cols, rows_out, cols_out, name):
    depth, rows, cols = w.shape
    shift = col_start % LANES
    base = col_start - shift
    tr = _largest_divisor(rows_out, 1024, 8)
    tc = _largest_divisor(cols_out, 2816, LANES)
    assert base % tc == 0 and (not shift or cols_out == n_cols)
    in_specs = [pl.BlockSpec((None, tr, tc), lambda l, i, j, o=base // tc: (l, i, j + o))]
    args = [w]
    if shift:
        in_specs.append(pl.BlockSpec((None, tr, LANES), lambda l, i, j, o=base // LANES, s=tc // LANES: (l, i, o + (j + 1) * s)))
        args.append(w)
    kern = functools.partial(_cast_kernel, shift=shift, valid_rows=rows if rows_out != rows else None,
                             valid_cols=n_cols if cols_out != n_cols else None)
    return pl.pallas_call(
        kern, grid=(depth, rows_out // tr, cols_out // tc), in_specs=in_specs,
        out_specs=pl.BlockSpec((None, tr, tc), lambda l, i, j: (l, i, j)),
        out_shape=jax.ShapeDtypeStruct((depth, rows_out, cols_out), BF16),
        compiler_params=_cparams(("parallel", "parallel", "parallel")), name=name)(*args)


def _cast_t_kernel(*refs, shift):
    if shift:
        a_ref, b_ref, o_ref = refs
        x = jnp.concatenate([a_ref[shift:, :], b_ref[...]], axis=0)
    else:
        a_ref, o_ref = refs
        x = a_ref[...]
    o_ref[...] = x.astype(o_ref.dtype).T


def cast_weights_t(wt, row_start, n_rows, name):
    depth, _, k = wt.shape
    tk = _largest_divisor(k, 2048, LANES)
    tn = _largest_divisor(n_rows, 1024, LANES)
    shift = row_start % tn
    base = row_start - shift
    assert shift % 8 == 0 and (not shift or tn % shift == 0)
    in_specs = [pl.BlockSpec((None, tn, tk), lambda l, i, j, o=base // tn: (l, j + o, i))]
    args = [wt]
    if shift:
        in_specs.append(pl.BlockSpec((None, shift, tk),
                                     lambda l, i, j, o=base // shift, s=tn // shift: (l, o + (j + 1) * s, i)))
        args.append(wt)
    return pl.pallas_call(
        functools.partial(_cast_t_kernel, shift=shift), grid=(depth, k // tk, n_rows // tn), in_specs=in_specs,
        out_specs=pl.BlockSpec((None, tk, tn), lambda l, i, j: (l, i, j)),
        out_shape=jax.ShapeDtypeStruct((depth, k, n_rows), BF16),
        compiler_params=_cparams(("parallel", "parallel", "parallel")), name=name)(*args)


def _gate_rows_kernel(a_ref, b_ref, o_ref):
    pad = jnp.zeros((LANES - a_ref.shape[0] - b_ref.shape[0], a_ref.shape[1]), F32)
    o_ref[...] = jnp.concatenate([a_ref[...], b_ref[...], pad], axis=0).T.astype(o_ref.dtype)


def gate_weights_t(wt, start_a, n_a, start_b, n_b):
    depth, _, k = wt.shape
    assert n_a % 8 == 0 and n_b % 8 == 0 and start_a % n_a == 0 and start_b % n_b == 0
    tk = _largest_divisor(k, 2048, LANES)
    return pl.pallas_call(
        _gate_rows_kernel, grid=(depth, k // tk),
        in_specs=[pl.BlockSpec((None, n_a, tk), lambda l, i, o=start_a // n_a: (l, o, i)),
                  pl.BlockSpec((None, n_b, tk), lambda l, i, o=start_b // n_b: (l, o, i))],
        out_specs=pl.BlockSpec((None, tk, LANES), lambda l, i: (l, i, 0)),
        out_shape=jax.ShapeDtypeStruct((depth, k, LANES), BF16),
        compiler_params=_cparams(("parallel", "parallel")), name="cast_w_small")(wt, wt)


def _norm_kernel(x_ref, g_ref, o_ref):
    x = x_ref[...]
    y = x * lax.rsqrt(jnp.mean(x * x, axis=-1, keepdims=True) + EPS)
    o_ref[...] = (y * g_ref[...]).astype(o_ref.dtype)


def rmsnorm_rows(x, g):
    m, d = x.shape
    tm = min(256, m)
    return pl.pallas_call(
        _norm_kernel, grid=(m // tm,),
        in_specs=[pl.BlockSpec((tm, d), lambda i: (i, 0)), pl.BlockSpec((1, d), lambda i: (0, 0))],
        out_specs=pl.BlockSpec((tm, d), lambda i: (i, 0)),
        out_shape=jax.ShapeDtypeStruct((m, d), BF16),
        compiler_params=_cparams(("parallel",)), name="rmsnorm")(x, g.reshape(1, d))


def _mm_kernel(*refs, n_lhs, ws, n_extra, n_const, epi, rider, w_valid_cols):
    n_pre = len(rider.prefetch) if rider else 0
    n_rin = len(rider.args) if rider else 0
    pre, refs = refs[:n_pre], refs[n_pre:]
    n_in = 2 * n_lhs + len(ws) + sum(has_bias for _, has_bias in ws) + 2 * n_extra + n_const
    it = iter(refs[:n_in])
    lhs_big = [next(it) for _ in range(n_lhs)]
    lhs_small = [next(it) for _ in range(n_lhs)]
    w_refs = [next(it) for _ in ws]
    b_refs = [next(it) if has_bias else None for (_, has_bias) in ws]
    ex_big = [next(it) for _ in range(n_extra)]
    ex_small = [next(it) for _ in range(n_extra)]
    consts = [next(it) for _ in range(n_const)]
    rider_in = refs[n_in:n_in + n_rin]
    out_big, out_small = refs[n_in + n_rin:n_in + n_rin + 2]
    rider_rest = refs[n_in + n_rin + 2:]

    def run(lhs, extras, out):
        accs = []
        for k, (li, _) in enumerate(ws):
            a = jnp.dot(lhs[li][...], w_refs[k][...], preferred_element_type=F32)
            if b_refs[k] is not None:
                a = a + b_refs[k][...]
            accs.append(a)
        out[...] = epi(accs, [e[...] for e in extras], [c[...] for c in consts]).astype(out.dtype)

    rider_refs = (*pre, *rider_in, *rider_rest)
    if w_valid_cols is not None:
        w_f32, w_refs = w_refs, rider_rest

        @pl.when(pl.program_id(1) == 0)
        def _():
            for src, dst in zip(w_f32, w_refs):
                w = src[...]
                cols = pl.program_id(0) * w.shape[1] + lax.broadcasted_iota(jnp.int32, w.shape, 1)
                dst[...] = jnp.where(cols < w_valid_cols, w, 0.0).astype(dst.dtype)
    if rider:
        rider.phases[0](*rider_refs)
    run(lhs_big, ex_big, out_big)
    if rider:
        rider.phases[1](*rider_refs)
        rider.phases[2](*rider_refs)

    @pl.when(pl.program_id(1) == 0)
    def _():
        run(lhs_small, ex_small, out_small)


class Rider:
    def __init__(self, grid, prefetch, in_specs, args, out_specs, out_shape, scratch_shapes, phases):
        self.grid, self.prefetch, self.in_specs, self.args = grid, prefetch, in_specs, args
        self.out_specs, self.out_shape, self.scratch_shapes, self.phases = out_specs, out_shape, scratch_shapes, phases


def fused_matmul(lhs, layer, weights, extras, consts, epi, n_cols, out_dtype, tm, tn, name, single_buffer_w=False,
                 rider=None, cast_weights_in_kernel=False):
    r = lhs[0][0].shape[0]
    s = lhs[0][1].shape[0]
    tm = min(tm, r)
    assert r % tm == 0 and n_cols % tn == 0
    grid = (n_cols // tn, r // tm)
    assert rider is None or rider.grid == grid

    def spec(shape, index, **kw):
        return pl.BlockSpec(shape, lambda j, i, *_: index(j, i), **kw)

    in_specs, args = [], []
    for big, _ in lhs:
        in_specs.append(spec((tm, big.shape[1]), lambda j, i: (i, 0)))
        args.append(big)
    for _, small in lhs:
        in_specs.append(spec(small.shape, lambda j, i: (0, 0)))
        args.append(small)
    w_mode = dict(pipeline_mode=pl.Buffered(1)) if single_buffer_w else {}
    for w, _, off, _ in weights:
        assert off % tn == 0
        in_specs.append(spec((None, w.shape[1], tn), functools.partial(lambda o, j, i: (layer, 0, j + o), off // tn),
                             **w_mode))
        args.append(w)
    for _, _, _, b in weights:
        if b is not None:
            assert b.shape == (1, n_cols)
            in_specs.append(spec((1, tn), lambda j, i: (0, j)))
            args.append(b)
    for big, _, off in extras:
        assert off % tn == 0
        in_specs.append(spec((tm, tn), functools.partial(lambda o, j, i: (i, j + o), off // tn)))
        args.append(big)
    for _, small, off in extras:
        in_specs.append(spec((s, tn), functools.partial(lambda o, j, i: (0, j + o), off // tn)))
        args.append(small)
    for c in consts:
        if c.shape == (1, n_cols):
            in_specs.append(spec((1, tn), lambda j, i: (0, j)))
        else:
            in_specs.append(spec(c.shape, functools.partial(lambda nd, j, i: (0,) * nd, c.ndim)))
        args.append(c)
    out_specs = [spec((tm, tn), lambda j, i: (i, j)), spec((s, tn), lambda j, i: (0, j))]
    out_shape = [jax.ShapeDtypeStruct((r, n_cols), out_dtype), jax.ShapeDtypeStruct((s, n_cols), out_dtype)]
    assert not (rider and cast_weights_in_kernel)
    kern = functools.partial(
        _mm_kernel, n_lhs=len(lhs), ws=[(li, b is not None) for _, li, _, b in weights],
        n_extra=len(extras), n_const=len(consts), epi=epi, rider=rider,
        w_valid_cols=weights[0][0].shape[2] if cast_weights_in_kernel else None)
    if rider is None:
        scratch = [pltpu.VMEM((w.shape[1], tn), BF16) for w, _, _, _ in weights] if cast_weights_in_kernel else []
        return pl.pallas_call(kern, grid=grid, in_specs=in_specs, out_specs=out_specs, out_shape=out_shape,
                              scratch_shapes=scratch,
                              compiler_params=_cparams(("parallel", "arbitrary")), name=name)(*args)
    grid_spec = pltpu.PrefetchScalarGridSpec(
        num_scalar_prefetch=len(rider.prefetch), grid=grid, in_specs=in_specs + rider.in_specs,
        out_specs=out_specs + rider.out_specs, scratch_shapes=rider.scratch_shapes)
    return pl.pallas_call(kern, grid_spec=grid_spec, out_shape=out_shape + rider.out_shape,
                          compiler_params=_cparams(("arbitrary", "arbitrary")), name=name)(
                              *rider.prefetch, *args, *rider.args)


def _epi_scale(accs, extras, consts):
    return accs[0] * consts[0]


def _epi_identity(accs, extras, consts):
    return accs[0]


def _epi_headnorm(accs, extras, consts):
    y, g = accs[0], consts[0]
    outs = []
    for h in range(y.shape[1] // LANES):
        yh = y[:, h * LANES:(h + 1) * LANES]
        outs.append(yh * lax.rsqrt(jnp.mean(yh * yh, axis=-1, keepdims=True) + EPS) * g)
    return jnp.concatenate(outs, axis=-1)


def _epi_gated2(accs, extras, consts):
    return (jax.nn.sigmoid(extras[0].astype(F32)) * accs[0] + jax.nn.sigmoid(extras[1].astype(F32)) * accs[1])


def _epi_residual(accs, extras, consts):
    return extras[0] + accs[0]


def _epi_swiglu(accs, extras, consts):
    return jax.nn.silu(accs[0]) * accs[1]


def _gates_kernel(h_ref, w_ref, b_ref, c0_ref, l_ref, cb_ref, cf_ref, lt_ref, cbt_ref, cft_ref, carry,
                  *, n_lin, n_gate, cs):
    @pl.when(pl.program_id(1) == 0)
    def _():
        carry[...] = c0_ref[...]

    z = jnp.dot(h_ref[...], w_ref[...], preferred_element_type=F32) + b_ref[...]
    lane = lax.broadcasted_iota(jnp.int32, z.shape, 1)
    logs = jnp.where(lane < n_lin, z, jnp.where(lane < n_gate, jax.nn.log_sigmoid(z), 0.0))
    tri = (lax.broadcasted_iota(jnp.int32, (cs, cs), 1) <= lax.broadcasted_iota(jnp.int32, (cs, cs), 0)).astype(F32)
    for s in range(z.shape[0] // cs):
        rows = slice(s * cs, (s + 1) * cs)
        lc = logs[rows]
        cb = jnp.dot(tri, lc, precision=lax.Precision.HIGHEST, preferred_element_type=F32)
        cf = cb + carry[...]
        carry[...] = cf[cs - 1:cs, :]
        l_ref[rows, :] = lc
        cb_ref[rows, :] = cb
        cf_ref[rows, :] = cf
        lt_ref[:, rows] = lc.T
        cbt_ref[:, rows] = cb.T
        cft_ref[:, rows] = cf.T


def gate_logs(h, w, layer, bias, carry0, n_seq, n_lin, n_gate):
    r, d = h.shape
    seq = r // n_seq
    tile = min(512, seq)
    nt = seq // tile
    cs = min(CHUNK, tile)
    row = pl.BlockSpec((tile, LANES), lambda b, t: (b * nt + t, 0))
    col = pl.BlockSpec((LANES, tile), lambda b, t: (0, b * nt + t))
    kern = functools.partial(_gates_kernel, n_lin=n_lin, n_gate=n_gate, cs=cs)
    return pl.pallas_call(
        kern, grid=(n_seq, nt),
        in_specs=[pl.BlockSpec((tile, d), lambda b, t: (b * nt + t, 0)),
                  pl.BlockSpec((None, d, LANES), lambda b, t: (layer, 0, 0)),
                  pl.BlockSpec((1, LANES), lambda b, t: (0, 0)),
                  pl.BlockSpec((1, LANES), lambda b, t: (0, 0))],
        out_specs=[row, row, row, col, col, col],
        out_shape=[jax.ShapeDtypeStruct((r, LANES), F32)] * 3 + [jax.ShapeDtypeStruct((LANES, r), F32)] * 3,
        scratch_shapes=[pltpu.VMEM((1, LANES), F32)],
        compiler_params=_cparams(("arbitrary", "arbitrary")), name="gate_logs")(h, w, bias, carry0)


def _mlstm_head(q, k, v, bcol, brow, icol, irow, c0, n0, m0, nv):
    ln = q.shape[0]
    g = bcol + m0
    d = bcol - brow + irow
    ti = lax.broadcasted_iota(jnp.int32, (ln, ln), 0)
    si = lax.broadcasted_iota(jnp.int32, (ln, ln), 1)
    d = jnp.where(si <= ti, d, -jnp.inf)
    m = jnp.maximum(g, jnp.max(d, axis=1, keepdims=True))
    w = jnp.exp(d - m)
    inter = jnp.exp(g - m)
    s = lax.dot_general(q, k, NT_DIMS, preferred_element_type=F32) * w
    num = (jnp.dot(s.astype(BF16), v, preferred_element_type=F32)
           + inter * jnp.dot(q, c0.astype(BF16), preferred_element_type=F32))
    den = jnp.sum(s, axis=1, keepdims=True) + inter * jnp.sum(q.astype(F32) * n0, axis=1, keepdims=True)
    h = num / jnp.maximum(jnp.abs(den), jnp.exp(-m))
    m_end = m[nv - 1:nv]
    sidx = lax.broadcasted_iota(jnp.int32, (ln, 1), 0)
    w_end = jnp.where(sidx < nv, jnp.exp(bcol[nv - 1:nv] - bcol + icol - m_end), 0.0)
    decay = jnp.exp(g[nv - 1:nv] - m_end)
    kw = k.astype(F32) * w_end
    c1 = decay * c0 + jnp.dot(kw.T.astype(BF16), v, preferred_element_type=F32)
    n1 = decay * n0 + jnp.sum(kw, axis=0, keepdims=True)
    return h, c1, n1, m_end


def _mlstm_kernel(q_ref, k_ref, v_ref, og_ref, g_ref, l_ref, cb_ref, lt_ref, cbt_ref, ci_ref, ni_ref, mi_ref,
                  hm_ref, co_ref, no_ref, mo_ref, c_s, n_s, m_s, *, n_heads, dqk, dv, nv):
    c = pl.program_id(1)

    @pl.when(c == 0)
    def _():
        c_s[...] = ci_ref[...]
        n_s[...] = ni_ref[...]
        m_s[...] = mi_ref[...]

    logs, cb, logs_t, cb_t = l_ref[...], cb_ref[...], lt_ref[...], cbt_ref[...]
    for h in range(n_heads):
        hh, c1, n1, m_end = _mlstm_head(
            q_ref[:, h * dqk:(h + 1) * dqk], k_ref[:, h * dqk:(h + 1) * dqk], v_ref[:, h * dv:(h + 1) * dv],
            cb[:, n_heads + h:n_heads + h + 1], cb_t[n_heads + h:n_heads + h + 1, :],
            logs[:, h:h + 1], logs_t[h:h + 1, :], c_s[h], n_s[h], m_s[h][:, 0:1], nv)
        c_s[h] = c1
        n_s[h] = n1
        m_s[h] = jnp.broadcast_to(m_end, (1, LANES))
        cols = slice(h * dv, (h + 1) * dv)
        hn = hh * lax.rsqrt(jnp.mean(hh * hh, axis=-1, keepdims=True) + EPS) * g_ref[:, cols]
        hm_ref[:, cols] = (hn * jax.nn.sigmoid(og_ref[:, cols].astype(F32))).astype(hm_ref.dtype)

    @pl.when(c == pl.num_programs(1) - 1)
    def _():
        co_ref[0] = c_s[...]
        no_ref[0] = n_s[...]
        mo_ref[0] = m_s[...]


def mlstm_chunks(qkv, og, mh_g, logs, cb, logs_t, cb_t, c_init, n_init, m_init, n_seq, nv):
    r = qkv.shape[0]
    n_heads, dqk, dv = c_init.shape
    nc = r // n_seq // CHUNK
    hq, hv = n_heads * dqk, n_heads * dv
    assert hv == 2 * hq
    rows = lambda b, c: b * nc + c
    kern = functools.partial(_mlstm_kernel, n_heads=n_heads, dqk=dqk, dv=dv, nv=nv)
    full3 = lambda shape: pl.BlockSpec(shape, lambda b, c: (0, 0, 0))
    return pl.pallas_call(
        kern, grid=(n_seq, nc),
        in_specs=[pl.BlockSpec((CHUNK, hq), lambda b, c: (rows(b, c), 0)),
                  pl.BlockSpec((CHUNK, hq), lambda b, c: (rows(b, c), 1)),
                  pl.BlockSpec((CHUNK, hv), lambda b, c: (rows(b, c), 1)),
                  pl.BlockSpec((CHUNK, hv), lambda b, c: (rows(b, c), 0)),
                  pl.BlockSpec((1, hv), lambda b, c: (0, 0)),
                  pl.BlockSpec((CHUNK, LANES), lambda b, c: (rows(b, c), 0)),
                  pl.BlockSpec((CHUNK, LANES), lambda b, c: (rows(b, c), 0)),
                  pl.BlockSpec((LANES, CHUNK), lambda b, c: (0, rows(b, c))),
                  pl.BlockSpec((LANES, CHUNK), lambda b, c: (0, rows(b, c))),
                  full3(c_init.shape), full3(n_init.shape), full3(m_init.shape)],
        out_specs=[pl.BlockSpec((CHUNK, hv), lambda b, c: (rows(b, c), 0)),
                   pl.BlockSpec((1,) + c_init.shape, lambda b, c: (b, 0, 0, 0)),
                   pl.BlockSpec((1,) + n_init.shape, lambda b, c: (b, 0, 0, 0)),
                   pl.BlockSpec((1,) + m_init.shape, lambda b, c: (b, 0, 0, 0))],
        out_shape=[jax.ShapeDtypeStruct((r, hv), BF16),
                   jax.ShapeDtypeStruct((n_seq,) + c_init.shape, F32),
                   jax.ShapeDtypeStruct((n_seq,) + n_init.shape, F32),
                   jax.ShapeDtypeStruct((n_seq,) + m_init.shape, F32)],
        scratch_shapes=[pltpu.VMEM(c_init.shape, F32), pltpu.VMEM(n_init.shape, F32), pltpu.VMEM(m_init.shape, F32)],
        compiler_params=_cparams(("arbitrary", "arbitrary")), name="mlstm_chunks")(
            qkv, qkv, qkv, og, mh_g, logs, cb, logs_t, cb_t, c_init, n_init, m_init)


def _mlstm_step_kernel(q_ref, k_ref, v_ref, og_ref, g_ref, li_ref, lf_ref, c_ref, n_ref, m_ref,
                       hm_ref, co_ref, no_ref, mo_ref, *, n_heads):
    q, k = q_ref[0].astype(F32), k_ref[0].astype(F32)
    v = v_ref[0].astype(F32)
    og = og_ref[0].astype(F32)
    pad = jnp.zeros((LANES - n_heads, q.shape[1]), F32)
    q_t = jnp.concatenate([q, pad], axis=0).T
    k_t = jnp.concatenate([k, pad], axis=0).T
    logi, logf, m0 = li_ref[0], lf_ref[0], m_ref[0]
    g = logf + m0
    m = jnp.maximum(g, logi)
    w = jnp.exp(logi - m)
    inter = jnp.exp(g - m)
    floor = jnp.exp(-m)
    for h in range(n_heads):
        c0, n0 = c_ref[0, h], n_ref[0, h]
        qh, kh, vh = q[h:h + 1], k[h:h + 1], v[h:h + 1]
        wh, ih = w[h:h + 1], inter[h:h + 1]
        s = jnp.sum(qh * kh, axis=1, keepdims=True) * wh
        num = s * vh + ih * jnp.sum(q_t[:, h:h + 1] * c0, axis=0, keepdims=True)
        den = s + ih * jnp.sum(qh * n0, axis=1, keepdims=True)
        hh = num / jnp.maximum(jnp.abs(den), floor[h:h + 1])
        co_ref[0, h] = ih * c0 + (wh * k_t[:, h:h + 1]) * vh
        no_ref[0, h] = ih * n0 + wh * kh
        hn = hh * lax.rsqrt(jnp.mean(hh * hh, axis=-1, keepdims=True) + EPS) * g_ref[h:h + 1]
        hm_ref[0, h:h + 1] = hn * jax.nn.sigmoid(og[h:h + 1])
    mo_ref[0] = m


def mlstm_step(q, k, v, og, mh_g, logi, logf, c0, n0, m0):
    nb, n_heads, dqk = q.shape
    dv = v.shape[2]
    b3 = lambda a: pl.BlockSpec((1,) + a.shape[1:], lambda i: (i,) + (0,) * (a.ndim - 1))
    outs = [jax.ShapeDtypeStruct((nb, n_heads, dv), F32), jax.ShapeDtypeStruct(c0.shape, F32),
            jax.ShapeDtypeStruct(n0.shape, F32), jax.ShapeDtypeStruct(m0.shape, F32)]
    return pl.pallas_call(
        functools.partial(_mlstm_step_kernel, n_heads=n_heads), grid=(nb,),
        in_specs=[b3(q), b3(k), b3(v), b3(og), pl.BlockSpec(mh_g.shape, lambda i: (0, 0)),
                  b3(logi), b3(logf), b3(c0), b3(n0), b3(m0)],
        out_specs=[b3(o) for o in outs], out_shape=outs,
        compiler_params=_cparams(("parallel",)), name="mlstm_step")(q, k, v, og, mh_g, logi, logf, c0, n0, m0)


def _fox_kernel(q_ref, k_ref, v_ref, km_ref, vm_ref, cf_ref, cft_ref, cmt_ref, o_ref, kb, vb,
                *, tq, tk, n_meta, scale, lane0, causal_meta):
    h = pl.program_id(1)
    seq = q_ref.shape[0]
    if not causal_meta:
        kb[...] = k_ref[...].astype(BF16)
        vb[...] = v_ref[...].astype(BF16)
    km = km_ref[...].astype(BF16)
    vm = vm_ref[...].astype(BF16)
    lane = lax.broadcasted_iota(jnp.int32, (tq, LANES), 1)
    row_id = lax.broadcasted_iota(jnp.int32, (tq, tk), 0)
    col_id = lax.broadcasted_iota(jnp.int32, (tq, tk), 1)

    def online(carry, s, vals):
        m, l, acc = carry
        mn = jnp.maximum(m, jnp.max(s, axis=1, keepdims=True))
        alpha = jnp.exp(m - mn)
        p = jnp.exp(s - mn)
        return (mn, alpha * l + jnp.sum(p, axis=1, keepdims=True),
                alpha * acc + jnp.dot(p.astype(BF16), vals, preferred_element_type=F32))

    for i in range(seq // tq):
        rows = slice(i * tq, (i + 1) * tq)
        q = q_ref[rows, :]
        fq = jnp.sum(jnp.where(lane == lane0 + h, cf_ref[rows, :], 0.0), axis=1, keepdims=True)
        s = lax.dot_general(q, km, NT_DIMS, preferred_element_type=F32) * scale + fq - cmt_ref[...]
        col = lax.broadcasted_iota(jnp.int32, s.shape, 1)
        ok = col < n_meta
        if causal_meta:
            ok = ok & (col <= lax.broadcasted_iota(jnp.int32, s.shape, 0))
        s = jnp.where(ok, s, -jnp.inf)
        m = jnp.max(s, axis=1, keepdims=True)
        p = jnp.exp(s - m)
        carry = (m, jnp.sum(p, axis=1, keepdims=True), jnp.dot(p.astype(BF16), vm, preferred_element_type=F32))
        if not causal_meta:
            for j in range(-(-(i + 1) * tq // tk)):
                cols = slice(j * tk, (j + 1) * tk)
                s = (lax.dot_general(q, kb[cols, :], NT_DIMS, preferred_element_type=F32) * scale
                     + fq - cft_ref[j:j + 1, :])
                if (j + 1) * tk - 1 > i * tq:
                    s = jnp.where(col_id + (j * tk - i * tq) <= row_id, s, -jnp.inf)
                carry = online(carry, s, vb[cols, :])
        o_ref[rows, :] = (carry[2] / carry[1]).astype(o_ref.dtype)


def fox_prompt(q, k, v, k_meta, v_meta, cf, cft, cmt, n_seq, n_meta, lane0, scale, causal_meta):
    r, width = q.shape
    dh = LANES
    n_heads = width // dh
    seq = r // n_seq
    tq, tk = min(FOX_TQ, seq), min(FOX_TK, seq)
    s_rows = k_meta.shape[0]
    cft3 = cft.reshape(LANES, n_seq, seq // tk, tk)
    cmt3 = cmt.reshape(LANES, 1, s_rows)
    kern = functools.partial(_fox_kernel, tq=tq, tk=tk, n_meta=n_meta, scale=scale, lane0=lane0,
                             causal_meta=causal_meta)
    blk = pl.BlockSpec((seq, dh), lambda b, h: (b, h))
    return pl.pallas_call(
        kern, grid=(n_seq, n_heads),
        in_specs=[blk, blk, blk,
                  pl.BlockSpec((s_rows, dh), lambda b, h: (0, h)),
                  pl.BlockSpec((s_rows, dh), lambda b, h: (0, h)),
                  pl.BlockSpec((seq, LANES), lambda b, h: (b, 0)),
                  pl.BlockSpec((None, None, seq // tk, tk), lambda b, h: (lane0 + h, b, 0, 0)),
                  pl.BlockSpec((None, 1, s_rows), lambda b, h: (lane0 + h, 0, 0))],
        out_specs=blk,
        out_shape=jax.ShapeDtypeStruct((r, width), BF16),
        scratch_shapes=[pltpu.VMEM((seq, dh), BF16), pltpu.VMEM((seq, dh), BF16)],
        compiler_params=_cparams(("parallel", "arbitrary")), name="fox_prompt")(
            q, k, v, k_meta, v_meta, cf, cft3, cmt3)


def _page_logs_kernel(x_ref, r_ref, t_ref, *, n_heads, rpp, sub_pages):
    hi = lax.Precision.HIGHEST
    li = lax.broadcasted_iota(jnp.int32, (LANES, LANES), 0)
    lo = lax.broadcasted_iota(jnp.int32, (LANES, LANES), 1)
    same_head = (li % n_heads) == (lo % n_heads)
    w_row = (same_head & (li // n_heads >= lo // n_heads)).astype(F32)
    w_all = same_head.astype(F32)
    sb = sub_pages * rpp
    ri = lax.broadcasted_iota(jnp.int32, (sb, sb), 0)
    ro = lax.broadcasted_iota(jnp.int32, (sb, sb), 1)
    later_rows = ((ro > ri) & (ro // rpp == ri // rpp)).astype(F32)
    pi = lax.broadcasted_iota(jnp.int32, (sub_pages * 8, sb), 0)
    po = lax.broadcasted_iota(jnp.int32, (sub_pages * 8, sb), 1)
    page_rows = (po // rpp == pi // 8).astype(F32)
    for s in range(x_ref.shape[0] // sb):
        x = x_ref[s * sb:(s + 1) * sb, :]
        row_tot = jnp.dot(x, w_all, precision=hi, preferred_element_type=F32)
        r_ref[s * sb:(s + 1) * sb, :] = (jnp.dot(x, w_row, precision=hi, preferred_element_type=F32)
                                         + jnp.dot(later_rows, row_tot, precision=hi, preferred_element_type=F32) - x)
        t_ref[s * sub_pages * 8:(s + 1) * sub_pages * 8, :] = jnp.dot(page_rows, row_tot, precision=hi,
                                                                      preferred_element_type=F32)


def page_logs(cache_lf):
    depth, n_pool, page, n_heads = cache_lf.shape
    assert LANES % n_heads == 0 and (page * n_heads) % (8 * LANES) == 0
    rpp = page * n_heads // LANES
    n_pages = depth * n_pool
    sub_pages = _largest_divisor(n_pages, max(1, LANES // rpp))
    tile_pages = sub_pages * _largest_divisor(n_pages // sub_pages, 8)
    x = cache_lf.reshape(n_pages * rpp, LANES)
    kern = functools.partial(_page_logs_kernel, n_heads=n_heads, rpp=rpp, sub_pages=sub_pages)
    r, t = pl.pallas_call(
        kern, grid=(n_pages // tile_pages,),
        in_specs=[pl.BlockSpec((tile_pages * rpp, LANES), lambda i: (i, 0))],
        out_specs=[pl.BlockSpec((tile_pages * rpp, LANES), lambda i: (i, 0)),
                   pl.BlockSpec((tile_pages * 8, LANES), lambda i: (i, 0))],
        out_shape=[jax.ShapeDtypeStruct((n_pages * rpp, LANES), F32), jax.ShapeDtypeStruct((n_pages * 8, LANES), F32)],
        compiler_params=_cparams(("parallel",)), name="page_logs")(x)
    return r.reshape(depth, n_pool, rpp, LANES), t.reshape(depth, n_pool, 8, LANES)


def _decode_refs(refs, pages_per_step):
    q_ref, kn_ref, vn_ref, fq_ref = refs[1:5]
    page_refs = refs[5:5 + 4 * pages_per_step]
    o_ref, m_s, l_s, acc, carry = refs[5 + 4 * pages_per_step:]
    return q_ref, kn_ref, vn_ref, fq_ref, page_refs, o_ref, m_s, l_s, acc, carry


def _fox_decode_init(*refs, scale, pages_per_step):
    q_ref, kn_ref, vn_ref, _, _, _, m_s, l_s, acc, carry = _decode_refs(refs, pages_per_step)

    @pl.when(pl.program_id(1) == 0)
    def _():
        m_s[...] = jnp.sum(q_ref[...].astype(F32) * kn_ref[...], axis=1, keepdims=True) * scale
        l_s[...] = jnp.ones_like(l_s)
        acc[...] = vn_ref[...]
        carry[...] = jnp.zeros_like(carry)


def _fox_decode_pages(*refs, n_heads, scale, pages_per_step):
    q_ref, _, _, fq_ref, page_refs, _, m_s, l_s, acc, carry = _decode_refs(refs, pages_per_step)
    q = q_ref[...]
    fq = fq_ref[...]
    later_pages = carry[0:1, :]
    m, l, a = m_s[...], l_s[...], acc[...]
    for u in range(pages_per_step):
        k_ref, v_ref, r_ref, t_ref = page_refs[4 * u:4 * u + 4]
        later = r_ref[...] + later_pages
        later_pages = later_pages + t_ref[0:1, :]
        bias = jnp.concatenate([jnp.broadcast_to(later[c:c + 1], (n_heads, LANES)) for c in range(later.shape[0])],
                               axis=1)
        s = lax.dot_general(q, k_ref[...].astype(BF16), NT_DIMS, preferred_element_type=F32) * scale + fq + bias
        own = (lax.broadcasted_iota(jnp.int32, s.shape, 1) % n_heads) == lax.broadcasted_iota(jnp.int32, s.shape, 0)
        s = jnp.where(own, s, -jnp.inf)
        mn = jnp.maximum(m, jnp.max(s, axis=1, keepdims=True))
        alpha = jnp.exp(m - mn)
        pr = jnp.exp(s - mn)
        l = alpha * l + jnp.sum(pr, axis=1, keepdims=True)
        a = alpha * a + jnp.dot(pr.astype(BF16), v_ref[...].astype(BF16), preferred_element_type=F32)
        m = mn
    m_s[...], l_s[...], acc[...] = m, l, a
    carry[...] = jnp.broadcast_to(later_pages, carry.shape)


def _fox_decode_finish(*refs, pages_per_step):
    _, _, _, _, _, o_ref, _, l_s, acc, _ = _decode_refs(refs, pages_per_step)

    @pl.when(pl.program_id(1) == pl.num_programs(1) - 1)
    def _():
        o_ref[...] = acc[...] / l_s[...]


def fox_decode_job(page_table, layer, q, k_new, v_new, fq, cache_k, cache_v, later_logs, page_totals, scale,
                   steps_per_seq=None):
    nb, n_pages = page_table.shape
    depth, n_pool, page, n_heads, dh = cache_k.shape
    rows = page * n_heads
    ck = cache_k.reshape(depth, n_pool, rows, dh)
    cv = cache_v.reshape(depth, n_pool, rows, dh)
    pps = n_pages // steps_per_seq if steps_per_seq else _largest_divisor(n_pages, 8)
    assert n_pages % pps == 0
    phases = (functools.partial(_fox_decode_init, scale=scale, pages_per_step=pps),
              functools.partial(_fox_decode_pages, n_heads=n_heads, scale=scale, pages_per_step=pps),
              functools.partial(_fox_decode_finish, pages_per_step=pps))
    per_b = lambda w: pl.BlockSpec((None, n_heads, w), lambda i, p, pt: (i, 0, 0))
    paged = lambda a, u: pl.BlockSpec((None, None) + a.shape[2:],
                                      lambda i, p, pt: (layer, pt[i, n_pages - 1 - (p * pps + u)], 0, 0))
    page_specs, page_args = [], []
    for u in range(pps):
        for a in (ck, cv, later_logs, page_totals):
            page_specs.append(paged(a, u))
            page_args.append(a)
    return Rider(
        grid=(nb, n_pages // pps), prefetch=[page_table],
        in_specs=[per_b(dh), per_b(dh), per_b(dh), per_b(1)] + page_specs, args=[q, k_new, v_new, fq] + page_args,
        out_specs=[per_b(dh)], out_shape=[jax.ShapeDtypeStruct((nb, n_heads, dh), F32)],
        scratch_shapes=[pltpu.VMEM((n_heads, 1), F32), pltpu.VMEM((n_heads, 1), F32), pltpu.VMEM((n_heads, dh), F32),
                        pltpu.VMEM((8, LANES), F32)],
        phases=phases)


def run_alone(job, name):
    def body(*refs):
        for phase in job.phases:
            phase(*refs)

    grid_spec = pltpu.PrefetchScalarGridSpec(
        num_scalar_prefetch=len(job.prefetch), grid=job.grid, in_specs=job.in_specs, out_specs=job.out_specs,
        scratch_shapes=job.scratch_shapes)
    return pl.pallas_call(body, grid_spec=grid_spec, out_shape=job.out_shape,
                          compiler_params=_cparams(("arbitrary",) * len(job.grid)), name=name)(
                              *job.prefetch, *job.args)


def kernel(x_prompt, x_sample, cache_k, cache_v, cache_lf, state_c, state_n, state_m, page_table, meta, norm1_g,
           w_in, b_in, q_norm_g, k_norm_g, mh_norm_g, p_a, p_b, w_out, norm2_g, w_gate, w_up, w_down):
    nb, seq, d = x_prompt.shape
    db = x_sample.shape[0]
    assert x_sample.shape[1] == 1
    n_meta = meta.shape[0]
    depth = w_in.shape[0]
    _, _, hm, dqk, dv = state_c.shape
    _, _, _, hf, dh = cache_k.shape
    d_ff = w_gate.shape[2]
    assert dh == LANES and dqk == LANES and seq % CHUNK == 0 and n_meta + db <= SMALL_ROWS
    assert 2 * hm + hf <= LANES
    r = nb * seq
    sizes = (hm * dqk, hm * dqk, hm * dv, hm * dv, hm, hm, hf * dh, hf * dh, hf * dh, hf, d, d)
    st = [0]
    for sz in sizes:
        st.append(st[-1] + sz)
    n_gate = 2 * hm + hf
    n_mqkv, n_mo, n_f = st[3], hm * dv, hf * dh
    d_ffp = d_ff + (-d_ff % 1024)
    scale_f = dh ** -0.5
    sample = slice(n_meta, n_meta + db)

    w_in_t = jnp.swapaxes(w_in, 1, 2)
    w_m = cast_weights_t(w_in_t, 0, st[4], "cast_w_m")
    w_f = cast_weights_t(w_in_t, st[6], 3 * n_f, "cast_w_f")
    w_g = cast_weights_t(w_in_t, st[10], 2 * d, "cast_w_g")
    w_small = gate_weights_t(w_in_t, st[4], 2 * hm, st[9], hf)
    pa = cast_weights(p_a, 0, d, p_a.shape[1], d, "cast_p_a")
    pb = cast_weights(p_b, 0, d, p_b.shape[1], d, "cast_p_b")
    wo = cast_weights(w_out, 0, d, d, d, "cast_w_out")
    wd = cast_weights(w_down, 0, d, d_ffp, d, "cast_w_down")

    xb = x_prompt.reshape(r, d)
    xs = jnp.concatenate([meta.astype(F32), x_sample.reshape(db, d),
                          jnp.zeros((SMALL_ROWS - n_meta - db, d), F32)], axis=0)
    qk_scale = jnp.concatenate([jnp.ones((1, hm * dqk), F32), jnp.full((1, hm * dqk), dqk ** -0.5, F32),
                                jnp.ones((1, hm * dv), F32)], axis=1)
    zero_row = jnp.zeros((1, LANES), F32)
    later_logs, page_totals = page_logs(cache_lf)
    outs = [[] for _ in range(12)]

    for l in range(depth):
        b_l = b_in[l].reshape(1, -1)
        b_small = jnp.concatenate([b_l[:, st[4]:st[6]], b_l[:, st[9]:st[10]], jnp.zeros((1, LANES - n_gate), F32)], axis=1)

        hb, hs = rmsnorm_rows(xb, norm1_g[l]), rmsnorm_rows(xs, norm1_g[l])
        lhs = [(hb, hs)]
        qkv, qkv_s = fused_matmul(lhs, l, [(w_m, 0, 0, b_l[:, :n_mqkv])], [], [qk_scale], _epi_scale, n_mqkv, BF16,
                                  1024, 1024, "in_qkv")
        og, og_s = fused_matmul(lhs, l, [(w_m, 0, n_mqkv, b_l[:, st[3]:st[4]])], [], [], _epi_identity, n_mo, BF16,
                                1024, 1024, "in_mo")
        fq, fq_s = fused_matmul(lhs, l, [(w_f, 0, 0, b_l[:, st[6]:st[7]])], [], [q_norm_g[l].reshape(1, dh)],
                                _epi_headnorm, n_f, BF16, 1024, 1024, "in_fq")
        fk, fk_s = fused_matmul(lhs, l, [(w_f, 0, n_f, b_l[:, st[7]:st[8]])], [], [k_norm_g[l].reshape(1, dh)],
                                _epi_headnorm, n_f, F32, 1024, 1024, "in_fk")
        fv, fv_s = fused_matmul(lhs, l, [(w_f, 0, 2 * n_f, b_l[:, st[8]:st[9]])], [], [], _epi_identity, n_f, F32,
                                1024, 1024, "in_fv")
        ls, cbs, _, lst, cbst, _ = gate_logs(hs, w_small, l, b_small, zero_row, 1, hm, n_gate)
        decode_in = (page_table, l, fq_s[sample].reshape(db, hf, dh), fk_s[sample].reshape(db, hf, dh),
                     fv_s[sample].reshape(db, hf, dh), ls[sample, 2 * hm:n_gate].reshape(db, hf, 1),
                     cache_k, cache_v, later_logs, page_totals, scale_f)
        page_steps = page_table.shape[1] // 4 if page_table.shape[1] % 4 == 0 else 0
        ride = 2 * d // 1024 == db and page_steps and r % page_steps == 0 and (r // page_steps) % LANES == 0
        if ride:
            gab, gab_s, hf_d = fused_matmul(lhs, l, [(w_g, 0, 0, b_l[:, st[10]:])], [], [], _epi_identity, 2 * d, BF16,
                                            r // page_steps, 1024, "in_gab_decode",
                                            rider=fox_decode_job(*decode_in, steps_per_seq=page_steps))
        else:
            gab, gab_s = fused_matmul(lhs, l, [(w_g, 0, 0, b_l[:, st[10]:])], [], [], _epi_identity, 2 * d, BF16,
                                      1024, 1024, "in_gab")
            hf_d, = run_alone(fox_decode_job(*decode_in), "fox_decode")
        lb, cbb, cfb, lbt, cbbt, cfbt = gate_logs(hb, w_small, l, b_small, cbs[n_meta - 1:n_meta], nb, hm, n_gate)

        mh_g = mh_norm_g[l].reshape(1, hm * dv)
        hm_meta, c_m, n_m, m_m = mlstm_chunks(
            qkv_s, og_s, mh_g, ls, cbs, lst, cbst, jnp.zeros((hm, dqk, dv), F32), jnp.zeros((hm, 1, dqk), F32),
            jnp.zeros((hm, 1, LANES), F32), 1, n_meta)
        hm_b, c_p, n_p, m_p = mlstm_chunks(qkv, og, mh_g, lb, cbb, lbt, cbbt, c_m[0], n_m[0], m_m[0], nb, CHUNK)
        hm_d, c_d, n_d, m_d = mlstm_step(
            qkv_s[sample, :hm * dqk].reshape(db, hm, dqk), qkv_s[sample, hm * dqk:2 * hm * dqk].reshape(db, hm, dqk),
            qkv_s[sample, 2 * hm * dqk:].reshape(db, hm, dv), og_s[sample].reshape(db, hm, dv),
            mh_g.reshape(hm, dv), ls[sample, :hm].reshape(db, hm, 1), ls[sample, hm:2 * hm].reshape(db, hm, 1),
            state_c[l], state_n[l].reshape(db, hm, 1, dqk), state_m[l].reshape(db, hm, 1))
        pad_rows = jnp.zeros((SMALL_ROWS - n_meta - db, hm * dv), BF16)
        hm_s = jnp.concatenate([hm_meta[:n_meta], hm_d.reshape(db, hm * dv).astype(BF16), pad_rows], axis=0)

        hf_meta = fox_prompt(fq_s, fk_s, fv_s, fk_s, fv_s, cbs, cbst, cbst, 1, n_meta, 2 * hm, scale_f, True)
        hf_b = fox_prompt(fq, fk, fv, fk_s, fv_s, cfb, cfbt, cbst, nb, n_meta, 2 * hm, scale_f, False)
        hf_s = jnp.concatenate([hf_meta[:n_meta], hf_d.reshape(db, hf * dh).astype(BF16), pad_rows], axis=0)

        u, u_s = fused_matmul([(hm_b, hm_s), (hf_b, hf_s)], l, [(pa, 0, 0, None), (pb, 1, 0, None)],
                              [(gab, gab_s, 0), (gab, gab_s, d)], [], _epi_gated2, d, BF16, 512, 1024, "merge_gate")
        x1, x1_s = fused_matmul([(u, u_s)], l, [(wo, 0, 0, None)], [(xb, xs, 0)], [], _epi_residual, d, F32,
                                512, 1024, "out_proj")
        h2, h2_s = rmsnorm_rows(x1, norm2_g[l]), rmsnorm_rows(x1_s, norm2_g[l])
        act, act_s = fused_matmul([(h2, h2_s)], l, [(w_gate, 0, 0, None), (w_up, 0, 0, None)], [], [], _epi_swiglu,
                                  d_ffp, BF16, 512, 512, "ffn_up", cast_weights_in_kernel=True)
        xb, xs = fused_matmul([(act, act_s)], l, [(wd, 0, 0, None)], [(x1, x1_s, 0)], [], _epi_residual, d, F32,
                              256, 1024, "ffn_down", single_buffer_w=True)

        outs[0].append((fk_s.reshape(SMALL_ROWS, hf, dh), fk.reshape(r, hf, dh)))
        outs[1].append((fv_s.reshape(SMALL_ROWS, hf, dh), fv.reshape(r, hf, dh)))
        outs[2].append((ls[:, 2 * hm:n_gate], lb[:, 2 * hm:n_gate]))
        outs[3].append(c_p)
        outs[4].append(n_p.reshape(nb, hm, dqk))
        outs[5].append(m_p[:, :, 0, 0])
        outs[6].append(fk_s[sample].reshape(db, 1, hf, dh))
        outs[7].append(fv_s[sample].reshape(db, 1, hf, dh))
        outs[8].append(ls[sample, 2 * hm:n_gate].reshape(db, 1, hf))
        outs[9].append(c_d)
        outs[10].append(n_d.reshape(db, hm, dqk))
        outs[11].append(m_d.reshape(db, hm))

    def with_meta(pairs):
        pieces = [p for small, big in pairs for b in range(nb) for p in (small[:n_meta], big[b * seq:(b + 1) * seq])]
        return jnp.concatenate(pieces, axis=0).reshape((depth, nb, n_meta + seq) + pieces[0].shape[1:])

    return ((xb.reshape(nb, seq, d), xs[sample].reshape(db, 1, d)) + tuple(with_meta(o) for o in outs[:3])
            + tuple(jnp.stack(o) for o in outs[3:]))
```

```python
import functools

import jax
import jax.numpy as jnp
from jax import lax
from jax.experimental import pallas as pl
from jax.experimental.pallas import tpu as pltpu

F32 = jnp.float32
BF16 = jnp.bfloat16
EPS = 1e-6
LANES = 128
CHUNK = 128
SMALL_ROWS = 128
FOX_TQ, FOX_TK = 512, 512
VMEM_LIMIT = 58 * 1024 * 1024
NT_DIMS = (((1,), (1,)), ((), ()))


def _cparams(sem):
    return pltpu.CompilerParams(dimension_semantics=sem, vmem_limit_bytes=VMEM_LIMIT)


def _largest_divisor(n, cap, multiple_of=1):
    return max(k for k in range(multiple_of, cap + 1, multiple_of) if n % k == 0)


def _cast_kernel(*refs, shift, valid_rows, valid_cols):
    if shift:
        a_ref, b_ref, o_ref = refs
        wide = jnp.concatenate([a_ref[...], b_ref[...]], axis=1)
        x = pltpu.roll(wide, wide.shape[1] - shift, 1)[:, :a_ref.shape[1]]
    else:
        a_ref, o_ref = refs
        x = a_ref[...]
    tr, tc = x.shape
    if valid_rows is not None:
        rows = pl.program_id(1) * tr + lax.broadcasted_iota(jnp.int32, x.shape, 0)
        x = jnp.where(rows < valid_rows, x, 0.0)
    if valid_cols is not None:
        cols = pl.program_id(2) * tc + lax.broadcasted_iota(jnp.int32, x.shape, 1)
        x = jnp.where(cols < valid_cols, x, 0.0)
    o_ref[...] = x.astype(o_ref.dtype)


def cast_weights(w, col_start, n_cols, rows_out, cols_out, name):
    depth, rows, cols = w.shape
    shift = col_start % LANES
    base = col_start - shift
    tr = _largest_divisor(rows_out, 1024, 8)
    tc = _largest_divisor(cols_out, 2816, LANES)
    assert base % tc == 0 and (not shift or cols_out == n_cols)
    in_specs = [pl.BlockSpec((None, tr, tc), lambda l, i, j, o=base // tc: (l, i, j + o))]
    args = [w]
    if shift:
        in_specs.append(pl.BlockSpec((None, tr, LANES), lambda l, i, j, o=base // LANES, s=tc // LANES: (l, i, o + (j + 1) * s)))
        args.append(w)
    kern = functools.partial(_cast_kernel, shift=shift, valid_rows=rows if rows_out != rows else None,
                             valid_cols=n_cols if cols_out != n_cols else None)
    return pl.pallas_call(
        kern, grid=(depth, rows_out // tr, cols_out // tc), in_specs=in_specs,
        out_specs=pl.BlockSpec((None, tr, tc), lambda l, i, j: (l, i, j)),
        out_shape=jax.ShapeDtypeStruct((depth, rows_out, cols_out), BF16),
        compiler_params=_cparams(("parallel", "parallel", "parallel")), name=name)(*args)


def _cast_t_kernel(*refs, shift):
    if shift:
        a_ref, b_ref, o_ref = refs
        x = jnp.concatenate([a_ref[shift:, :], b_ref[...]], axis=0)
    else:
        a_ref, o_ref = refs
        x = a_ref[...]
    o_ref[...] = x.astype(o_ref.dtype).T


def cast_weights_t(wt, row_start, n_rows, name):
    depth, _, k = wt.shape
    tk = _largest_divisor(k, 2048, LANES)
    tn = _largest_divisor(n_rows, 1024, LANES)
    shift = row_start % tn
    base = row_start - shift
    assert shift % 8 == 0 and (not shift or tn % shift == 0)
    in_specs = [pl.BlockSpec((None, tn, tk), lambda l, i, j, o=base // tn: (l, j + o, i))]
    args = [wt]
    if shift:
        in_specs.append(pl.BlockSpec((None, shift, tk),
                                     lambda l, i, j, o=base // shift, s=tn // shift: (l, o + (j + 1) * s, i)))
        args.append(wt)
    return pl.pallas_call(
        functools.partial(_cast_t_kernel, shift=shift), grid=(depth, k // tk, n_rows // tn), in_specs=in_specs,
        out_specs=pl.BlockSpec((None, tk, tn), lambda l, i, j: (l, i, j)),
        out_shape=jax.ShapeDtypeStruct((depth, k, n_rows), BF16),
        compiler_params=_cparams(("parallel", "parallel", "parallel")), name=name)(*args)


def _gate_rows_kernel(a_ref, b_ref, o_ref):
    pad = jnp.zeros((LANES - a_ref.shape[0] - b_ref.shape[0], a_ref.shape[1]), F32)
    o_ref[...] = jnp.concatenate([a_ref[...], b_ref[...], pad], axis=0).T.astype(o_ref.dtype)


def gate_weights_t(wt, start_a, n_a, start_b, n_b):
    depth, _, k = wt.shape
    assert n_a % 8 == 0 and n_b % 8 == 0 and start_a % n_a == 0 and start_b % n_b == 0
    tk = _largest_divisor(k, 2048, LANES)
    return pl.pallas_call(
        _gate_rows_kernel, grid=(depth, k // tk),
        in_specs=[pl.BlockSpec((None, n_a, tk), lambda l, i, o=start_a // n_a: (l, o, i)),
                  pl.BlockSpec((None, n_b, tk), lambda l, i, o=start_b // n_b: (l, o, i))],
        out_specs=pl.BlockSpec((None, tk, LANES), lambda l, i: (l, i, 0)),
        out_shape=jax.ShapeDtypeStruct((depth, k, LANES), BF16),
        compiler_params=_cparams(("parallel", "parallel")), name="cast_w_small")(wt, wt)


def _norm_kernel(x_ref, g_ref, o_ref):
    x = x_ref[...]
    y = x * lax.rsqrt(jnp.mean(x * x, axis=-1, keepdims=True) + EPS)
    o_ref[...] = (y * g_ref[...]).astype(o_ref.dtype)


def rmsnorm_rows(x, g):
    m, d = x.shape
    tm = min(256, m)
    return pl.pallas_call(
        _norm_kernel, grid=(m // tm,),
        in_specs=[pl.BlockSpec((tm, d), lambda i: (i, 0)), pl.BlockSpec((1, d), lambda i: (0, 0))],
        out_specs=pl.BlockSpec((tm, d), lambda i: (i, 0)),
        out_shape=jax.ShapeDtypeStruct((m, d), BF16),
        compiler_params=_cparams(("parallel",)), name="rmsnorm")(x, g.reshape(1, d))


def _mm_kernel(*refs, n_lhs, ws, n_extra, n_const, epi, rider, w_valid_cols):
    n_pre = len(rider.prefetch) if rider else 0
    n_rin = len(rider.args) if rider else 0
    pre, refs = refs[:n_pre], refs[n_pre:]
    n_in = 2 * n_lhs + len(ws) + sum(has_bias for _, has_bias in ws) + 2 * n_extra + n_const
    it = iter(refs[:n_in])
    lhs_big = [next(it) for _ in range(n_lhs)]
    lhs_small = [next(it) for _ in range(n_lhs)]
    w_refs = [next(it) for _ in ws]
    b_refs = [next(it) if has_bias else None for (_, has_bias) in ws]
    ex_big = [next(it) for _ in range(n_extra)]
    ex_small = [next(it) for _ in range(n_extra)]
    consts = [next(it) for _ in range(n_const)]
    rider_in = refs[n_in:n_in + n_rin]
    out_big, out_small = refs[n_in + n_rin:n_in + n_rin + 2]
    rider_rest = refs[n_in + n_rin + 2:]

    def run(lhs, extras, out):
        accs = []
        for k, (li, _) in enumerate(ws):
            a = jnp.dot(lhs[li][...], w_refs[k][...], preferred_element_type=F32)
            if b_refs[k] is not None:
                a = a + b_refs[k][...]
            accs.append(a)
        out[...] = epi(accs, [e[...] for e in extras], [c[...] for c in consts]).astype(out.dtype)

    rider_refs = (*pre, *rider_in, *rider_rest)
    if w_valid_cols is not None:
        w_f32, w_refs = w_refs, rider_rest

        @pl.when(pl.program_id(1) == 0)
        def _():
            for src, dst in zip(w_f32, w_refs):
                w = src[...]
                cols = pl.program_id(0) * w.shape[1] + lax.broadcasted_iota(jnp.int32, w.shape, 1)
                dst[...] = jnp.where(cols < w_valid_cols, w, 0.0).astype(dst.dtype)
    if rider:
        rider.phases[0](*rider_refs)
    run(lhs_big, ex_big, out_big)
    if rider:
        rider.phases[1](*rider_refs)
        rider.phases[2](*rider_refs)

    @pl.when(pl.program_id(1) == 0)
    def _():
        run(lhs_small, ex_small, out_small)


class Rider:
    def __init__(self, grid, prefetch, in_specs, args, out_specs, out_shape, scratch_shapes, phases):
        self.grid, self.prefetch, self.in_specs, self.args = grid, prefetch, in_specs, args
        self.out_specs, self.out_shape, self.scratch_shapes, self.phases = out_specs, out_shape, scratch_shapes, phases


def _cast_rows_kernel(x_ref, o_ref, *, grid_rows, valid_rows):
    x = x_ref[...]
    if valid_rows is not None:
        step = pl.program_id(0) * grid_rows + pl.program_id(1)
        rows = step * x.shape[0] + lax.broadcasted_iota(jnp.int32, x.shape, 0)
        x = jnp.where(rows < valid_rows, x, 0.0)
    o_ref[...] = x.astype(o_ref.dtype)


def cast_rider(w, layer, grid, rows_out):
    _, rows, cols = w.shape
    steps = grid[0] * grid[1]
    tr = rows_out // steps
    assert rows_out % steps == 0 and tr % 16 == 0
    main = functools.partial(_cast_rows_kernel, grid_rows=grid[1], valid_rows=rows if rows_out != rows else None)
    nop = lambda *refs: None
    last = (rows - 1) // tr
    return Rider(
        grid=grid, prefetch=[],
        in_specs=[pl.BlockSpec((None, tr, cols), lambda j, i: (layer, jnp.minimum(j * grid[1] + i, last), 0))],
        args=[w],
        out_specs=[pl.BlockSpec((None, tr, cols), lambda j, i: (0, j * grid[1] + i, 0))],
        out_shape=[jax.ShapeDtypeStruct((1, rows_out, cols), BF16)], scratch_shapes=[], phases=(nop, main, nop))


def fused_matmul(lhs, layer, weights, extras, consts, epi, n_cols, out_dtype, tm, tn, name, single_buffer_w=False,
                 rider=None, cast_weights_in_kernel=False):
    r = lhs[0][0].shape[0]
    s = lhs[0][1].shape[0]
    tm = min(tm, r)
    assert r % tm == 0 and n_cols % tn == 0
    grid = (n_cols // tn, r // tm)
    assert rider is None or rider.grid == grid

    def spec(shape, index, **kw):
        return pl.BlockSpec(shape, lambda j, i, *_: index(j, i), **kw)

    in_specs, args = [], []
    for big, _ in lhs:
        in_specs.append(spec((tm, big.shape[1]), lambda j, i: (i, 0)))
        args.append(big)
    for _, small in lhs:
        in_specs.append(spec(small.shape, lambda j, i: (0, 0)))
        args.append(small)
    w_mode = dict(pipeline_mode=pl.Buffered(1)) if single_buffer_w else {}
    for w, _, off, _ in weights:
        assert off % tn == 0
        in_specs.append(spec((None, w.shape[1], tn), functools.partial(lambda o, j, i: (layer, 0, j + o), off // tn),
                             **w_mode))
        args.append(w)
    for _, _, _, b in weights:
        if b is not None:
            assert b.shape == (1, n_cols)
            in_specs.append(spec((1, tn), lambda j, i: (0, j)))
            args.append(b)
    for big, _, off in extras:
        assert off % tn == 0
        in_specs.append(spec((tm, tn), functools.partial(lambda o, j, i: (i, j + o), off // tn)))
        args.append(big)
    for _, small, off in extras:
        in_specs.append(spec((s, tn), functools.partial(lambda o, j, i: (0, j + o), off // tn)))
        args.append(small)
    for c in consts:
        if c.shape == (1, n_cols):
            in_specs.append(spec((1, tn), lambda j, i: (0, j)))
        else:
            in_specs.append(spec(c.shape, functools.partial(lambda nd, j, i: (0,) * nd, c.ndim)))
        args.append(c)
    out_specs = [spec((tm, tn), lambda j, i: (i, j)), spec((s, tn), lambda j, i: (0, j))]
    out_shape = [jax.ShapeDtypeStruct((r, n_cols), out_dtype), jax.ShapeDtypeStruct((s, n_cols), out_dtype)]
    assert not (rider and cast_weights_in_kernel)
    kern = functools.partial(
        _mm_kernel, n_lhs=len(lhs), ws=[(li, b is not None) for _, li, _, b in weights],
        n_extra=len(extras), n_const=len(consts), epi=epi, rider=rider,
        w_valid_cols=weights[0][0].shape[2] if cast_weights_in_kernel else None)
    if rider is None:
        scratch = [pltpu.VMEM((w.shape[1], tn), BF16) for w, _, _, _ in weights] if cast_weights_in_kernel else []
        return pl.pallas_call(kern, grid=grid, in_specs=in_specs, out_specs=out_specs, out_shape=out_shape,
                              scratch_shapes=scratch,
                              compiler_params=_cparams(("parallel", "arbitrary")), name=name)(*args)
    grid_spec = pltpu.PrefetchScalarGridSpec(
        num_scalar_prefetch=len(rider.prefetch), grid=grid, in_specs=in_specs + rider.in_specs,
        out_specs=out_specs + rider.out_specs, scratch_shapes=rider.scratch_shapes)
    return pl.pallas_call(kern, grid_spec=grid_spec, out_shape=out_shape + rider.out_shape,
                          compiler_params=_cparams(("arbitrary", "arbitrary")), name=name)(
                              *rider.prefetch, *args, *rider.args)


def _epi_scale(accs, extras, consts):
    return accs[0] * consts[0]


def _epi_identity(accs, extras, consts):
    return accs[0]


def _epi_headnorm(accs, extras, consts):
    y, g = accs[0], consts[0]
    outs = []
    for h in range(y.shape[1] // LANES):
        yh = y[:, h * LANES:(h + 1) * LANES]
        outs.append(yh * lax.rsqrt(jnp.mean(yh * yh, axis=-1, keepdims=True) + EPS) * g)
    return jnp.concatenate(outs, axis=-1)


def _epi_gated2(accs, extras, consts):
    return (jax.nn.sigmoid(extras[0].astype(F32)) * accs[0] + jax.nn.sigmoid(extras[1].astype(F32)) * accs[1])


def _epi_residual(accs, extras, consts):
    return extras[0] + accs[0]


def _epi_swiglu(accs, extras, consts):
    return jax.nn.silu(accs[0]) * accs[1]


def _gates_kernel(h_ref, w_ref, b_ref, c0_ref, l_ref, cb_ref, cf_ref, lt_ref, cbt_ref, cft_ref, carry,
                  *, n_lin, n_gate, cs):
    @pl.when(pl.program_id(1) == 0)
    def _():
        carry[...] = c0_ref[...]

    z = jnp.dot(h_ref[...], w_ref[...], preferred_element_type=F32) + b_ref[...]
    lane = lax.broadcasted_iota(jnp.int32, z.shape, 1)
    logs = jnp.where(lane < n_lin, z, jnp.where(lane < n_gate, jax.nn.log_sigmoid(z), 0.0))
    tri = (lax.broadcasted_iota(jnp.int32, (cs, cs), 1) <= lax.broadcasted_iota(jnp.int32, (cs, cs), 0)).astype(F32)
    for s in range(z.shape[0] // cs):
        rows = slice(s * cs, (s + 1) * cs)
        lc = logs[rows]
        cb = jnp.dot(tri, lc, precision=lax.Precision.HIGHEST, preferred_element_type=F32)
        cf = cb + carry[...]
        carry[...] = cf[cs - 1:cs, :]
        l_ref[rows, :] = lc
        cb_ref[rows, :] = cb
        cf_ref[rows, :] = cf
        lt_ref[:, rows] = lc.T
        cbt_ref[:, rows] = cb.T
        cft_ref[:, rows] = cf.T


def gate_logs(h, w, layer, bias, carry0, n_seq, n_lin, n_gate):
    r, d = h.shape
    seq = r // n_seq
    tile = min(512, seq)
    nt = seq // tile
    cs = min(CHUNK, tile)
    row = pl.BlockSpec((tile, LANES), lambda b, t: (b * nt + t, 0))
    col = pl.BlockSpec((LANES, tile), lambda b, t: (0, b * nt + t))
    kern = functools.partial(_gates_kernel, n_lin=n_lin, n_gate=n_gate, cs=cs)
    return pl.pallas_call(
        kern, grid=(n_seq, nt),
        in_specs=[pl.BlockSpec((tile, d), lambda b, t: (b * nt + t, 0)),
                  pl.BlockSpec((None, d, LANES), lambda b, t: (layer, 0, 0)),
                  pl.BlockSpec((1, LANES), lambda b, t: (0, 0)),
                  pl.BlockSpec((1, LANES), lambda b, t: (0, 0))],
        out_specs=[row, row, row, col, col, col],
        out_shape=[jax.ShapeDtypeStruct((r, LANES), F32)] * 3 + [jax.ShapeDtypeStruct((LANES, r), F32)] * 3,
        scratch_shapes=[pltpu.VMEM((1, LANES), F32)],
        compiler_params=_cparams(("arbitrary", "arbitrary")), name="gate_logs")(h, w, bias, carry0)


def _cast_block(x_ref, o_ref):
    x = x_ref[...].astype(o_ref.dtype)
    pad = o_ref.shape[1] - x.shape[1]
    if pad:
        x = jnp.concatenate([x, jnp.zeros((x.shape[0], pad), o_ref.dtype)], axis=1)
    o_ref[...] = x


def cast_block_specs(w, layer, cols_out, grid):
    _, rows, cols = w.shape
    steps = grid[0] * grid[1]
    tr = rows // steps
    assert rows % steps == 0 and tr % 16 == 0 and cols % LANES == 0 and cols_out % LANES == 0
    return (pl.BlockSpec((None, tr, cols), lambda a, b: (layer, a * grid[1] + b, 0)),
            pl.BlockSpec((None, tr, cols_out), lambda a, b: (0, a * grid[1] + b, 0)),
            jax.ShapeDtypeStruct((1, rows, cols_out), BF16))


def _mlstm_head(q, k, v, bcol, brow, icol, irow, c0, n0, m0, nv):
    ln = q.shape[0]
    g = bcol + m0
    d = bcol - brow + irow
    ti = lax.broadcasted_iota(jnp.int32, (ln, ln), 0)
    si = lax.broadcasted_iota(jnp.int32, (ln, ln), 1)
    d = jnp.where(si <= ti, d, -jnp.inf)
    m = jnp.maximum(g, jnp.max(d, axis=1, keepdims=True))
    w = jnp.exp(d - m)
    inter = jnp.exp(g - m)
    s = lax.dot_general(q, k, NT_DIMS, preferred_element_type=F32) * w
    num = (jnp.dot(s.astype(BF16), v, preferred_element_type=F32)
           + inter * jnp.dot(q, c0.astype(BF16), preferred_element_type=F32))
    den = jnp.sum(s, axis=1, keepdims=True) + inter * jnp.sum(q.astype(F32) * n0, axis=1, keepdims=True)
    h = num / jnp.maximum(jnp.abs(den), jnp.exp(-m))
    m_end = m[nv - 1:nv]
    sidx = lax.broadcasted_iota(jnp.int32, (ln, 1), 0)
    w_end = jnp.where(sidx < nv, jnp.exp(bcol[nv - 1:nv] - bcol + icol - m_end), 0.0)
    decay = jnp.exp(g[nv - 1:nv] - m_end)
    kw = k.astype(F32) * w_end
    c1 = decay * c0 + jnp.dot(kw.T.astype(BF16), v, preferred_element_type=F32)
    n1 = decay * n0 + jnp.sum(kw, axis=0, keepdims=True)
    return h, c1, n1, m_end


def _mlstm_kernel(q_ref, k_ref, v_ref, og_ref, g_ref, l_ref, cb_ref, lt_ref, cbt_ref, ci_ref, ni_ref, mi_ref,
                  *refs, n_heads, dqk, dv, nv, with_cast):
    if with_cast:
        w_ref, hm_ref, co_ref, no_ref, mo_ref, wo_ref, c_s, n_s, m_s = refs
    else:
        hm_ref, co_ref, no_ref, mo_ref, c_s, n_s, m_s = refs
    c = pl.program_id(1)

    @pl.when(c == 0)
    def _():
        c_s[...] = ci_ref[...]
        n_s[...] = ni_ref[...]
        m_s[...] = mi_ref[...]

    logs, cb, logs_t, cb_t = l_ref[...], cb_ref[...], lt_ref[...], cbt_ref[...]
    for h in range(n_heads):
        hh, c1, n1, m_end = _mlstm_head(
            q_ref[:, h * dqk:(h + 1) * dqk], k_ref[:, h * dqk:(h + 1) * dqk], v_ref[:, h * dv:(h + 1) * dv],
            cb[:, n_heads + h:n_heads + h + 1], cb_t[n_heads + h:n_heads + h + 1, :],
            logs[:, h:h + 1], logs_t[h:h + 1, :], c_s[h], n_s[h], m_s[h][:, 0:1], nv)
        c_s[h] = c1
        n_s[h] = n1
        m_s[h] = jnp.broadcast_to(m_end, (1, LANES))
        cols = slice(h * dv, (h + 1) * dv)
        hn = hh * lax.rsqrt(jnp.mean(hh * hh, axis=-1, keepdims=True) + EPS) * g_ref[:, cols]
        hm_ref[:, cols] = (hn * jax.nn.sigmoid(og_ref[:, cols].astype(F32))).astype(hm_ref.dtype)
    if with_cast:
        _cast_block(w_ref, wo_ref)

    @pl.when(c == pl.num_programs(1) - 1)
    def _():
        co_ref[0] = c_s[...]
        no_ref[0] = n_s[...]
        mo_ref[0] = m_s[...]


def mlstm_chunks(qkv, og, mh_g, logs, cb, logs_t, cb_t, c_init, n_init, m_init, n_seq, nv, cast=None):
    r = qkv.shape[0]
    n_heads, dqk, dv = c_init.shape
    nc = r // n_seq // CHUNK
    hq, hv = n_heads * dqk, n_heads * dv
    assert hv == 2 * hq
    rows = lambda b, c: b * nc + c
    kern = functools.partial(_mlstm_kernel, n_heads=n_heads, dqk=dqk, dv=dv, nv=nv, with_cast=cast is not None)
    full3 = lambda shape: pl.BlockSpec(shape, lambda b, c: (0, 0, 0))
    cast_in, cast_out, cast_shape = cast_block_specs(*cast, (n_seq, nc)) if cast else (None, None, None)
    return pl.pallas_call(
        kern, grid=(n_seq, nc),
        in_specs=[pl.BlockSpec((CHUNK, hq), lambda b, c: (rows(b, c), 0)),
                  pl.BlockSpec((CHUNK, hq), lambda b, c: (rows(b, c), 1)),
                  pl.BlockSpec((CHUNK, hv), lambda b, c: (rows(b, c), 1)),
                  pl.BlockSpec((CHUNK, hv), lambda b, c: (rows(b, c), 0)),
                  pl.BlockSpec((1, hv), lambda b, c: (0, 0)),
                  pl.BlockSpec((CHUNK, LANES), lambda b, c: (rows(b, c), 0)),
                  pl.BlockSpec((CHUNK, LANES), lambda b, c: (rows(b, c), 0)),
                  pl.BlockSpec((LANES, CHUNK), lambda b, c: (0, rows(b, c))),
                  pl.BlockSpec((LANES, CHUNK), lambda b, c: (0, rows(b, c))),
                  full3(c_init.shape), full3(n_init.shape), full3(m_init.shape)] + ([cast_in] if cast else []),
        out_specs=[pl.BlockSpec((CHUNK, hv), lambda b, c: (rows(b, c), 0)),
                   pl.BlockSpec((1,) + c_init.shape, lambda b, c: (b, 0, 0, 0)),
                   pl.BlockSpec((1,) + n_init.shape, lambda b, c: (b, 0, 0, 0)),
                   pl.BlockSpec((1,) + m_init.shape, lambda b, c: (b, 0, 0, 0))] + ([cast_out] if cast else []),
        out_shape=[jax.ShapeDtypeStruct((r, hv), BF16),
                   jax.ShapeDtypeStruct((n_seq,) + c_init.shape, F32),
                   jax.ShapeDtypeStruct((n_seq,) + n_init.shape, F32),
                   jax.ShapeDtypeStruct((n_seq,) + m_init.shape, F32)] + ([cast_shape] if cast else []),
        scratch_shapes=[pltpu.VMEM(c_init.shape, F32), pltpu.VMEM(n_init.shape, F32), pltpu.VMEM(m_init.shape, F32)],
        compiler_params=_cparams(("arbitrary", "arbitrary")), name="mlstm_chunks")(
            qkv, qkv, qkv, og, mh_g, logs, cb, logs_t, cb_t, c_init, n_init, m_init, *([cast[0]] if cast else []))


def _mlstm_step_kernel(q_ref, k_ref, v_ref, og_ref, g_ref, li_ref, lf_ref, c_ref, n_ref, m_ref,
                       hm_ref, co_ref, no_ref, mo_ref, *, n_heads):
    q, k = q_ref[0].astype(F32), k_ref[0].astype(F32)
    v = v_ref[0].astype(F32)
    og = og_ref[0].astype(F32)
    pad = jnp.zeros((LANES - n_heads, q.shape[1]), F32)
    q_t = jnp.concatenate([q, pad], axis=0).T
    k_t = jnp.concatenate([k, pad], axis=0).T
    logi, logf, m0 = li_ref[0], lf_ref[0], m_ref[0]
    g = logf + m0
    m = jnp.maximum(g, logi)
    w = jnp.exp(logi - m)
    inter = jnp.exp(g - m)
    floor = jnp.exp(-m)
    for h in range(n_heads):
        c0, n0 = c_ref[0, h], n_ref[0, h]
        qh, kh, vh = q[h:h + 1], k[h:h + 1], v[h:h + 1]
        wh, ih = w[h:h + 1], inter[h:h + 1]
        s = jnp.sum(qh * kh, axis=1, keepdims=True) * wh
        num = s * vh + ih * jnp.sum(q_t[:, h:h + 1] * c0, axis=0, keepdims=True)
        den = s + ih * jnp.sum(qh * n0, axis=1, keepdims=True)
        hh = num / jnp.maximum(jnp.abs(den), floor[h:h + 1])
        co_ref[0, h] = ih * c0 + (wh * k_t[:, h:h + 1]) * vh
        no_ref[0, h] = ih * n0 + wh * kh
        hn = hh * lax.rsqrt(jnp.mean(hh * hh, axis=-1, keepdims=True) + EPS) * g_ref[h:h + 1]
        hm_ref[0, h:h + 1] = hn * jax.nn.sigmoid(og[h:h + 1])
    mo_ref[0] = m


def mlstm_step(q, k, v, og, mh_g, logi, logf, c0, n0, m0):
    nb, n_heads, dqk = q.shape
    dv = v.shape[2]
    b3 = lambda a: pl.BlockSpec((1,) + a.shape[1:], lambda i: (i,) + (0,) * (a.ndim - 1))
    outs = [jax.ShapeDtypeStruct((nb, n_heads, dv), F32), jax.ShapeDtypeStruct(c0.shape, F32),
            jax.ShapeDtypeStruct(n0.shape, F32), jax.ShapeDtypeStruct(m0.shape, F32)]
    return pl.pallas_call(
        functools.partial(_mlstm_step_kernel, n_heads=n_heads), grid=(nb,),
        in_specs=[b3(q), b3(k), b3(v), b3(og), pl.BlockSpec(mh_g.shape, lambda i: (0, 0)),
                  b3(logi), b3(logf), b3(c0), b3(n0), b3(m0)],
        out_specs=[b3(o) for o in outs], out_shape=outs,
        compiler_params=_cparams(("parallel",)), name="mlstm_step")(q, k, v, og, mh_g, logi, logf, c0, n0, m0)


def _fox_kernel(q_ref, k_ref, v_ref, km_ref, vm_ref, cf_ref, cft_ref, cmt_ref, *refs,
                tq, tk, n_meta, scale, lane0, causal_meta, with_cast):
    if with_cast:
        w_ref, o_ref, wo_ref, kb, vb = refs
        _cast_block(w_ref, wo_ref)
    else:
        o_ref, kb, vb = refs
    h = pl.program_id(1)
    seq = q_ref.shape[0]
    if not causal_meta:
        kb[...] = k_ref[...].astype(BF16)
        vb[...] = v_ref[...].astype(BF16)
    km = km_ref[...].astype(BF16)
    vm = vm_ref[...].astype(BF16)
    lane = lax.broadcasted_iota(jnp.int32, (tq, LANES), 1)
    row_id = lax.broadcasted_iota(jnp.int32, (tq, tk), 0)
    col_id = lax.broadcasted_iota(jnp.int32, (tq, tk), 1)

    def online(carry, s, vals):
        m, l, acc = carry
        mn = jnp.maximum(m, jnp.max(s, axis=1, keepdims=True))
        alpha = jnp.exp(m - mn)
        p = jnp.exp(s - mn)
        return (mn, alpha * l + jnp.sum(p, axis=1, keepdims=True),
                alpha * acc + jnp.dot(p.astype(BF16), vals, preferred_element_type=F32))

    for i in range(seq // tq):
        rows = slice(i * tq, (i + 1) * tq)
        q = q_ref[rows, :]
        fq = jnp.sum(jnp.where(lane == lane0 + h, cf_ref[rows, :], 0.0), axis=1, keepdims=True)
        s = lax.dot_general(q, km, NT_DIMS, preferred_element_type=F32) * scale + fq - cmt_ref[...]
        col = lax.broadcasted_iota(jnp.int32, s.shape, 1)
        ok = col < n_meta
        if causal_meta:
            ok = ok & (col <= lax.broadcasted_iota(jnp.int32, s.shape, 0))
        s = jnp.where(ok, s, -jnp.inf)
        m = jnp.max(s, axis=1, keepdims=True)
        p = jnp.exp(s - m)
        carry = (m, jnp.sum(p, axis=1, keepdims=True), jnp.dot(p.astype(BF16), vm, preferred_element_type=F32))
        if not causal_meta:
            for j in range(-(-(i + 1) * tq // tk)):
                cols = slice(j * tk, (j + 1) * tk)
                s = (lax.dot_general(q, kb[cols, :], NT_DIMS, preferred_element_type=F32) * scale
                     + fq - cft_ref[j:j + 1, :])
                if (j + 1) * tk - 1 > i * tq:
                    s = jnp.where(col_id + (j * tk - i * tq) <= row_id, s, -jnp.inf)
                carry = online(carry, s, vb[cols, :])
        o_ref[rows, :] = (carry[2] / carry[1]).astype(o_ref.dtype)


def fox_prompt(q, k, v, k_meta, v_meta, cf, cft, cmt, n_seq, n_meta, lane0, scale, causal_meta, cast=None):
    r, width = q.shape
    dh = LANES
    n_heads = width // dh
    seq = r // n_seq
    tq, tk = min(FOX_TQ, seq), min(FOX_TK, seq)
    s_rows = k_meta.shape[0]
    cft3 = cft.reshape(LANES, n_seq, seq // tk, tk)
    cmt3 = cmt.reshape(LANES, 1, s_rows)
    kern = functools.partial(_fox_kernel, tq=tq, tk=tk, n_meta=n_meta, scale=scale, lane0=lane0,
                             causal_meta=causal_meta, with_cast=cast is not None)
    blk = pl.BlockSpec((seq, dh), lambda b, h: (b, h))
    cast_in, cast_out, cast_shape = cast_block_specs(*cast, (n_seq, n_heads)) if cast else (None, None, None)
    out_shape = jax.ShapeDtypeStruct((r, width), BF16)
    return pl.pallas_call(
        kern, grid=(n_seq, n_heads),
        in_specs=[blk, blk, blk,
                  pl.BlockSpec((s_rows, dh), lambda b, h: (0, h)),
                  pl.BlockSpec((s_rows, dh), lambda b, h: (0, h)),
                  pl.BlockSpec((seq, LANES), lambda b, h: (b, 0)),
                  pl.BlockSpec((None, None, seq // tk, tk), lambda b, h: (lane0 + h, b, 0, 0)),
                  pl.BlockSpec((None, 1, s_rows), lambda b, h: (lane0 + h, 0, 0))] + ([cast_in] if cast else []),
        out_specs=[blk, cast_out] if cast else blk,
        out_shape=[out_shape, cast_shape] if cast else out_shape,
        scratch_shapes=[pltpu.VMEM((seq, dh), BF16), pltpu.VMEM((seq, dh), BF16)],
        compiler_params=_cparams(("parallel", "arbitrary")), name="fox_prompt")(
            q, k, v, k_meta, v_meta, cf, cft3, cmt3, *([cast[0]] if cast else []))


def _page_logs_kernel(x_ref, r_ref, t_ref, *, n_heads, rpp, sub_pages):
    hi = lax.Precision.HIGHEST
    li = lax.broadcasted_iota(jnp.int32, (LANES, LANES), 0)
    lo = lax.broadcasted_iota(jnp.int32, (LANES, LANES), 1)
    same_head = (li % n_heads) == (lo % n_heads)
    w_row = (same_head & (li // n_heads >= lo // n_heads)).astype(F32)
    w_all = same_head.astype(F32)
    sb = sub_pages * rpp
    ri = lax.broadcasted_iota(jnp.int32, (sb, sb), 0)
    ro = lax.broadcasted_iota(jnp.int32, (sb, sb), 1)
    later_rows = ((ro > ri) & (ro // rpp == ri // rpp)).astype(F32)
    pi = lax.broadcasted_iota(jnp.int32, (sub_pages * 8, sb), 0)
    po = lax.broadcasted_iota(jnp.int32, (sub_pages * 8, sb), 1)
    page_rows = (po // rpp == pi // 8).astype(F32)
    for s in range(x_ref.shape[0] // sb):
        x = x_ref[s * sb:(s + 1) * sb, :]
        row_tot = jnp.dot(x, w_all, precision=hi, preferred_element_type=F32)
        r_ref[s * sb:(s + 1) * sb, :] = (jnp.dot(x, w_row, precision=hi, preferred_element_type=F32)
                                         + jnp.dot(later_rows, row_tot, precision=hi, preferred_element_type=F32) - x)
        t_ref[s * sub_pages * 8:(s + 1) * sub_pages * 8, :] = jnp.dot(page_rows, row_tot, precision=hi,
                                                                      preferred_element_type=F32)


def page_logs(cache_lf):
    depth, n_pool, page, n_heads = cache_lf.shape
    assert LANES % n_heads == 0 and (page * n_heads) % (8 * LANES) == 0
    rpp = page * n_heads // LANES
    n_pages = depth * n_pool
    sub_pages = _largest_divisor(n_pages, max(1, LANES // rpp))
    tile_pages = sub_pages * _largest_divisor(n_pages // sub_pages, 8)
    x = cache_lf.reshape(n_pages * rpp, LANES)
    kern = functools.partial(_page_logs_kernel, n_heads=n_heads, rpp=rpp, sub_pages=sub_pages)
    r, t = pl.pallas_call(
        kern, grid=(n_pages // tile_pages,),
        in_specs=[pl.BlockSpec((tile_pages * rpp, LANES), lambda i: (i, 0))],
        out_specs=[pl.BlockSpec((tile_pages * rpp, LANES), lambda i: (i, 0)),
                   pl.BlockSpec((tile_pages * 8, LANES), lambda i: (i, 0))],
        out_shape=[jax.ShapeDtypeStruct((n_pages * rpp, LANES), F32), jax.ShapeDtypeStruct((n_pages * 8, LANES), F32)],
        compiler_params=_cparams(("parallel",)), name="page_logs")(x)
    return r.reshape(depth, n_pool, rpp, LANES), t.reshape(depth, n_pool, 8, LANES)


def _decode_refs(refs, pages_per_step):
    q_ref, kn_ref, vn_ref, fq_ref = refs[1:5]
    page_refs = refs[5:5 + 4 * pages_per_step]
    o_ref, m_s, l_s, acc, carry = refs[5 + 4 * pages_per_step:]
    return q_ref, kn_ref, vn_ref, fq_ref, page_refs, o_ref, m_s, l_s, acc, carry


def _fox_decode_init(*refs, scale, pages_per_step):
    q_ref, kn_ref, vn_ref, _, _, _, m_s, l_s, acc, carry = _decode_refs(refs, pages_per_step)

    @pl.when(pl.program_id(1) == 0)
    def _():
        m_s[...] = jnp.sum(q_ref[...].astype(F32) * kn_ref[...], axis=1, keepdims=True) * scale
        l_s[...] = jnp.ones_like(l_s)
        acc[...] = vn_ref[...]
        carry[...] = jnp.zeros_like(carry)


def _fox_decode_pages(*refs, n_heads, scale, pages_per_step):
    q_ref, _, _, fq_ref, page_refs, _, m_s, l_s, acc, carry = _decode_refs(refs, pages_per_step)
    q = q_ref[...]
    fq = fq_ref[...]
    later_pages = carry[0:1, :]
    m, l, a = m_s[...], l_s[...], acc[...]
    for u in range(pages_per_step):
        k_ref, v_ref, r_ref, t_ref = page_refs[4 * u:4 * u + 4]
        later = r_ref[...] + later_pages
        later_pages = later_pages + t_ref[0:1, :]
        bias = jnp.concatenate([jnp.broadcast_to(later[c:c + 1], (n_heads, LANES)) for c in range(later.shape[0])],
                               axis=1)
        s = lax.dot_general(q, k_ref[...].astype(BF16), NT_DIMS, preferred_element_type=F32) * scale + fq + bias
        own = (lax.broadcasted_iota(jnp.int32, s.shape, 1) % n_heads) == lax.broadcasted_iota(jnp.int32, s.shape, 0)
        s = jnp.where(own, s, -jnp.inf)
        mn = jnp.maximum(m, jnp.max(s, axis=1, keepdims=True))
        alpha = jnp.exp(m - mn)
        pr = jnp.exp(s - mn)
        l = alpha * l + jnp.sum(pr, axis=1, keepdims=True)
        a = alpha * a + jnp.dot(pr.astype(BF16), v_ref[...].astype(BF16), preferred_element_type=F32)
        m = mn
    m_s[...], l_s[...], acc[...] = m, l, a
    carry[...] = jnp.broadcast_to(later_pages, carry.shape)


def _fox_decode_finish(*refs, pages_per_step):
    _, _, _, _, _, o_ref, _, l_s, acc, _ = _decode_refs(refs, pages_per_step)

    @pl.when(pl.program_id(1) == pl.num_programs(1) - 1)
    def _():
        o_ref[...] = acc[...] / l_s[...]


def fox_decode_job(page_table, layer, q, k_new, v_new, fq, cache_k, cache_v, later_logs, page_totals, scale,
                   steps_per_seq=None):
    nb, n_pages = page_table.shape
    depth, n_pool, page, n_heads, dh = cache_k.shape
    rows = page * n_heads
    ck = cache_k.reshape(depth, n_pool, rows, dh)
    cv = cache_v.reshape(depth, n_pool, rows, dh)
    pps = n_pages // steps_per_seq if steps_per_seq else _largest_divisor(n_pages, 8)
    assert n_pages % pps == 0
    phases = (functools.partial(_fox_decode_init, scale=scale, pages_per_step=pps),
              functools.partial(_fox_decode_pages, n_heads=n_heads, scale=scale, pages_per_step=pps),
              functools.partial(_fox_decode_finish, pages_per_step=pps))
    per_b = lambda w: pl.BlockSpec((None, n_heads, w), lambda i, p, pt: (i, 0, 0))
    paged = lambda a, u: pl.BlockSpec((None, None) + a.shape[2:],
                                      lambda i, p, pt: (layer, pt[i, n_pages - 1 - (p * pps + u)], 0, 0))
    page_specs, page_args = [], []
    for u in range(pps):
        for a in (ck, cv, later_logs, page_totals):
            page_specs.append(paged(a, u))
            page_args.append(a)
    return Rider(
        grid=(nb, n_pages // pps), prefetch=[page_table],
        in_specs=[per_b(dh), per_b(dh), per_b(dh), per_b(1)] + page_specs, args=[q, k_new, v_new, fq] + page_args,
        out_specs=[per_b(dh)], out_shape=[jax.ShapeDtypeStruct((nb, n_heads, dh), F32)],
        scratch_shapes=[pltpu.VMEM((n_heads, 1), F32), pltpu.VMEM((n_heads, 1), F32), pltpu.VMEM((n_heads, dh), F32),
                        pltpu.VMEM((8, LANES), F32)],
        phases=phases)


def run_alone(job, name):
    def body(*refs):
        for phase in job.phases:
            phase(*refs)

    grid_spec = pltpu.PrefetchScalarGridSpec(
        num_scalar_prefetch=len(job.prefetch), grid=job.grid, in_specs=job.in_specs, out_specs=job.out_specs,
        scratch_shapes=job.scratch_shapes)
    return pl.pallas_call(body, grid_spec=grid_spec, out_shape=job.out_shape,
                          compiler_params=_cparams(("arbitrary",) * len(job.grid)), name=name)(
                              *job.prefetch, *job.args)


def kernel(x_prompt, x_sample, cache_k, cache_v, cache_lf, state_c, state_n, state_m, page_table, meta, norm1_g,
           w_in, b_in, q_norm_g, k_norm_g, mh_norm_g, p_a, p_b, w_out, norm2_g, w_gate, w_up, w_down):
    nb, seq, d = x_prompt.shape
    db = x_sample.shape[0]
    assert x_sample.shape[1] == 1
    n_meta = meta.shape[0]
    depth = w_in.shape[0]
    _, _, hm, dqk, dv = state_c.shape
    _, _, _, hf, dh = cache_k.shape
    d_ff = w_gate.shape[2]
    assert dh == LANES and dqk == LANES and seq % CHUNK == 0 and n_meta + db <= SMALL_ROWS
    assert 2 * hm + hf <= LANES
    r = nb * seq
    sizes = (hm * dqk, hm * dqk, hm * dv, hm * dv, hm, hm, hf * dh, hf * dh, hf * dh, hf, d, d)
    st = [0]
    for sz in sizes:
        st.append(st[-1] + sz)
    n_gate = 2 * hm + hf
    n_mqkv, n_mo, n_f = st[3], hm * dv, hf * dh
    d_ffp = d_ff + (-d_ff % 1024)
    scale_f = dh ** -0.5
    sample = slice(n_meta, n_meta + db)

    w_in_t = jnp.swapaxes(w_in, 1, 2)
    w_m = cast_weights_t(w_in_t, 0, st[4], "cast_w_m")
    w_f = cast_weights_t(w_in_t, st[6], 3 * n_f, "cast_w_f")
    w_g = cast_weights_t(w_in_t, st[10], 2 * d, "cast_w_g")
    w_small = gate_weights_t(w_in_t, st[4], 2 * hm, st[9], hf)

    xb = x_prompt.reshape(r, d)
    xs = jnp.concatenate([meta.astype(F32), x_sample.reshape(db, d),
                          jnp.zeros((SMALL_ROWS - n_meta - db, d), F32)], axis=0)
    qk_scale = jnp.concatenate([jnp.ones((1, hm * dqk), F32), jnp.full((1, hm * dqk), dqk ** -0.5, F32),
                                jnp.ones((1, hm * dv), F32)], axis=1)
    zero_row = jnp.zeros((1, LANES), F32)
    later_logs, page_totals = page_logs(cache_lf)
    outs = [[] for _ in range(12)]

    for l in range(depth):
        b_l = b_in[l].reshape(1, -1)
        b_small = jnp.concatenate([b_l[:, st[4]:st[6]], b_l[:, st[9]:st[10]], jnp.zeros((1, LANES - n_gate), F32)], axis=1)

        hb, hs = rmsnorm_rows(xb, norm1_g[l]), rmsnorm_rows(xs, norm1_g[l])
        lhs = [(hb, hs)]
        tm_in = min(1024, r)
        qkv, qkv_s, wo = fused_matmul(lhs, l, [(w_m, 0, 0, b_l[:, :n_mqkv])], [], [qk_scale], _epi_scale, n_mqkv, BF16,
                                      tm_in, 1024, "in_qkv",
                                      rider=cast_rider(w_out, l, (n_mqkv // 1024, r // tm_in), d))
        og, og_s, pa = fused_matmul(lhs, l, [(w_m, 0, n_mqkv, b_l[:, st[3]:st[4]])], [], [], _epi_identity, n_mo, BF16,
                                    tm_in, 1024, "in_mo", rider=cast_rider(p_a, l, (n_mo // 1024, r // tm_in), n_mo))
        fq, fq_s, pb = fused_matmul(lhs, l, [(w_f, 0, 0, b_l[:, st[6]:st[7]])], [], [q_norm_g[l].reshape(1, dh)],
                                    _epi_headnorm, n_f, BF16, tm_in, 1024, "in_fq",
                                    rider=cast_rider(p_b, l, (n_f // 1024, r // tm_in), n_f))
        fk, fk_s = fused_matmul(lhs, l, [(w_f, 0, n_f, b_l[:, st[7]:st[8]])], [], [k_norm_g[l].reshape(1, dh)],
                                _epi_headnorm, n_f, F32, 1024, 1024, "in_fk")
        fv, fv_s = fused_matmul(lhs, l, [(w_f, 0, 2 * n_f, b_l[:, st[8]:st[9]])], [], [], _epi_identity, n_f, F32,
                                1024, 1024, "in_fv")
        ls, cbs, _, lst, cbst, _ = gate_logs(hs, w_small, l, b_small, zero_row, 1, hm, n_gate)
        decode_in = (page_table, l, fq_s[sample].reshape(db, hf, dh), fk_s[sample].reshape(db, hf, dh),
                     fv_s[sample].reshape(db, hf, dh), ls[sample, 2 * hm:n_gate].reshape(db, hf, 1),
                     cache_k, cache_v, later_logs, page_totals, scale_f)
        page_steps = page_table.shape[1] // 4 if page_table.shape[1] % 4 == 0 else 0
        ride = 2 * d // 1024 == db and page_steps and r % page_steps == 0 and (r // page_steps) % LANES == 0
        if ride:
            gab, gab_s, hf_d = fused_matmul(lhs, l, [(w_g, 0, 0, b_l[:, st[10]:])], [], [], _epi_identity, 2 * d, BF16,
                                            r // page_steps, 1024, "in_gab_decode",
                                            rider=fox_decode_job(*decode_in, steps_per_seq=page_steps))
        else:
            gab, gab_s = fused_matmul(lhs, l, [(w_g, 0, 0, b_l[:, st[10]:])], [], [], _epi_identity, 2 * d, BF16,
                                      1024, 1024, "in_gab")
            hf_d, = run_alone(fox_decode_job(*decode_in), "fox_decode")
        lb, cbb, cfb, lbt, cbbt, cfbt = gate_logs(hb, w_small, l, b_small, cbs[n_meta - 1:n_meta], nb, hm, n_gate)

        mh_g = mh_norm_g[l].reshape(1, hm * dv)
        hm_meta, c_m, n_m, m_m = mlstm_chunks(
            qkv_s, og_s, mh_g, ls, cbs, lst, cbst, jnp.zeros((hm, dqk, dv), F32), jnp.zeros((hm, 1, dqk), F32),
            jnp.zeros((hm, 1, LANES), F32), 1, n_meta)
        hm_b, c_p, n_p, m_p, wg = mlstm_chunks(qkv, og, mh_g, lb, cbb, lbt, cbbt, c_m[0], n_m[0], m_m[0], nb, CHUNK,
                                               cast=(w_gate, l, d_ffp))
        hm_d, c_d, n_d, m_d = mlstm_step(
            qkv_s[sample, :hm * dqk].reshape(db, hm, dqk), qkv_s[sample, hm * dqk:2 * hm * dqk].reshape(db, hm, dqk),
            qkv_s[sample, 2 * hm * dqk:].reshape(db, hm, dv), og_s[sample].reshape(db, hm, dv),
            mh_g.reshape(hm, dv), ls[sample, :hm].reshape(db, hm, 1), ls[sample, hm:2 * hm].reshape(db, hm, 1),
            state_c[l], state_n[l].reshape(db, hm, 1, dqk), state_m[l].reshape(db, hm, 1))
        pad_rows = jnp.zeros((SMALL_ROWS - n_meta - db, hm * dv), BF16)
        hm_s = jnp.concatenate([hm_meta[:n_meta], hm_d.reshape(db, hm * dv).astype(BF16), pad_rows], axis=0)

        hf_meta = fox_prompt(fq_s, fk_s, fv_s, fk_s, fv_s, cbs, cbst, cbst, 1, n_meta, 2 * hm, scale_f, True)
        hf_b, wu = fox_prompt(fq, fk, fv, fk_s, fv_s, cfb, cfbt, cbst, nb, n_meta, 2 * hm, scale_f, False,
                              cast=(w_up, l, d_ffp))
        hf_s = jnp.concatenate([hf_meta[:n_meta], hf_d.reshape(db, hf * dh).astype(BF16), pad_rows], axis=0)

        tm_mg = min(512, r)
        u, u_s, wd = fused_matmul([(hm_b, hm_s), (hf_b, hf_s)], 0, [(pa, 0, 0, None), (pb, 1, 0, None)],
                                  [(gab, gab_s, 0), (gab, gab_s, d)], [], _epi_gated2, d, BF16, tm_mg, 1024,
                                  "merge_gate", rider=cast_rider(w_down, l, (d // 1024, r // tm_mg), d_ffp))
        x1, x1_s = fused_matmul([(u, u_s)], 0, [(wo, 0, 0, None)], [(xb, xs, 0)], [], _epi_residual, d, F32,
                                512, 1024, "out_proj")
        h2, h2_s = rmsnorm_rows(x1, norm2_g[l]), rmsnorm_rows(x1_s, norm2_g[l])
        act, act_s = fused_matmul([(h2, h2_s)], 0, [(wg, 0, 0, None), (wu, 0, 0, None)], [], [], _epi_swiglu,
                                  d_ffp, BF16, 512, 1024, "ffn_up")
        xb, xs = fused_matmul([(act, act_s)], 0, [(wd, 0, 0, None)], [(x1, x1_s, 0)], [], _epi_residual, d, F32,
                              256, 1024, "ffn_down", single_buffer_w=True)

        outs[0].append((fk_s.reshape(SMALL_ROWS, hf, dh), fk.reshape(r, hf, dh)))
        outs[1].append((fv_s.reshape(SMALL_ROWS, hf, dh), fv.reshape(r, hf, dh)))
        outs[2].append((ls[:, 2 * hm:n_gate], lb[:, 2 * hm:n_gate]))
        outs[3].append(c_p)
        outs[4].append(n_p.reshape(nb, hm, dqk))
        outs[5].append(m_p[:, :, 0, 0])
        outs[6].append(fk_s[sample].reshape(db, 1, hf, dh))
        outs[7].append(fv_s[sample].reshape(db, 1, hf, dh))
        outs[8].append(ls[sample, 2 * hm:n_gate].reshape(db, 1, hf))
        outs[9].append(c_d)
        outs[10].append(n_d.reshape(db, hm, dqk))
        outs[11].append(m_d.reshape(db, hm))

    def with_meta(pairs):
        pieces = [p for small, big in pairs for b in range(nb) for p in (small[:n_meta], big[b * seq:(b + 1) * seq])]
        return jnp.concatenate(pieces, axis=0).reshape((depth, nb, n_meta + seq) + pieces[0].shape[1:])

    return ((xb.reshape(nb, seq, d), xs[sample].reshape(db, 1, d)) + tuple(with_meta(o) for o in outs[:3])
            + tuple(jnp.stack(o) for o in outs[3:]))
```

```python
import functools

import jax
import jax.numpy as jnp
from jax import lax
from jax.experimental import pallas as pl
from jax.experimental.pallas import tpu as pltpu

F32 = jnp.float32
BF16 = jnp.bfloat16
EPS = 1e-6
LANES = 128
CHUNK = 128
SMALL_ROWS = 128
FOX_TQ, FOX_TK = 512, 512
VMEM_LIMIT = 58 * 1024 * 1024
NT_DIMS = (((1,), (1,)), ((), ()))
LOG2E = 1.4426950408889634


def _cparams(sem):
    return pltpu.CompilerParams(dimension_semantics=sem, vmem_limit_bytes=VMEM_LIMIT)


def _largest_divisor(n, cap, multiple_of=1):
    return max(k for k in range(multiple_of, cap + 1, multiple_of) if n % k == 0)


def _cast_kernel(*refs, shift, valid_rows, valid_cols):
    if shift:
        a_ref, b_ref, o_ref = refs
        wide = jnp.concatenate([a_ref[...], b_ref[...]], axis=1)
        x = pltpu.roll(wide, wide.shape[1] - shift, 1)[:, :a_ref.shape[1]]
    else:
        a_ref, o_ref = refs
        x = a_ref[...]
    tr, tc = x.shape
    if valid_rows is not None:
        rows = pl.program_id(1) * tr + lax.broadcasted_iota(jnp.int32, x.shape, 0)
        x = jnp.where(rows < valid_rows, x, 0.0)
    if valid_cols is not None:
        cols = pl.program_id(2) * tc + lax.broadcasted_iota(jnp.int32, x.shape, 1)
        x = jnp.where(cols < valid_cols, x, 0.0)
    o_ref[...] = x.astype(o_ref.dtype)


def cast_weights(w, col_start, n_cols, rows_out, cols_out, name):
    depth, rows, cols = w.shape
    shift = col_start % LANES
    base = col_start - shift
    tr = _largest_divisor(rows_out, 1024, 8)
    tc = _largest_divisor(cols_out, 2816, LANES)
    assert base % tc == 0 and (not shift or cols_out == n_cols)
    in_specs = [pl.BlockSpec((None, tr, tc), lambda l, i, j, o=base // tc: (l, i, j + o))]
    args = [w]
    if shift:
        in_specs.append(pl.BlockSpec((None, tr, LANES), lambda l, i, j, o=base // LANES, s=tc // LANES: (l, i, o + (j + 1) * s)))
        args.append(w)
    kern = functools.partial(_cast_kernel, shift=shift, valid_rows=rows if rows_out != rows else None,
                             valid_cols=n_cols if cols_out != n_cols else None)
    return pl.pallas_call(
        kern, grid=(depth, rows_out // tr, cols_out // tc), in_specs=in_specs,
        out_specs=pl.BlockSpec((None, tr, tc), lambda l, i, j: (l, i, j)),
        out_shape=jax.ShapeDtypeStruct((depth, rows_out, cols_out), BF16),
        compiler_params=_cparams(("parallel", "parallel", "parallel")), name=name)(*args)


def _cast_t_kernel(*refs, shift):
    if shift:
        a_ref, b_ref, o_ref = refs
        x = jnp.concatenate([a_ref[shift:, :], b_ref[...]], axis=0)
    else:
        a_ref, o_ref = refs
        x = a_ref[...]
    o_ref[...] = x.astype(o_ref.dtype).T


def cast_weights_t(wt, row_start, n_rows, name):
    depth, _, k = wt.shape
    tk = _largest_divisor(k, 2048, LANES)
    tn = _largest_divisor(n_rows, 1024, LANES)
    shift = row_start % tn
    base = row_start - shift
    assert shift % 8 == 0 and (not shift or tn % shift == 0)
    in_specs = [pl.BlockSpec((None, tn, tk), lambda l, i, j, o=base // tn: (l, j + o, i))]
    args = [wt]
    if shift:
        in_specs.append(pl.BlockSpec((None, shift, tk),
                                     lambda l, i, j, o=base // shift, s=tn // shift: (l, o + (j + 1) * s, i)))
        args.append(wt)
    return pl.pallas_call(
        functools.partial(_cast_t_kernel, shift=shift), grid=(depth, k // tk, n_rows // tn), in_specs=in_specs,
        out_specs=pl.BlockSpec((None, tk, tn), lambda l, i, j: (l, i, j)),
        out_shape=jax.ShapeDtypeStruct((depth, k, n_rows), BF16),
        compiler_params=_cparams(("parallel", "parallel", "parallel")), name=name)(*args)


def _gate_rows_kernel(a_ref, b_ref, o_ref):
    pad = jnp.zeros((LANES - a_ref.shape[0] - b_ref.shape[0], a_ref.shape[1]), F32)
    o_ref[...] = jnp.concatenate([a_ref[...], b_ref[...], pad], axis=0).T.astype(o_ref.dtype)


def gate_weights_t(wt, start_a, n_a, start_b, n_b):
    depth, _, k = wt.shape
    assert n_a % 8 == 0 and n_b % 8 == 0 and start_a % n_a == 0 and start_b % n_b == 0
    tk = _largest_divisor(k, 2048, LANES)
    return pl.pallas_call(
        _gate_rows_kernel, grid=(depth, k // tk),
        in_specs=[pl.BlockSpec((None, n_a, tk), lambda l, i, o=start_a // n_a: (l, o, i)),
                  pl.BlockSpec((None, n_b, tk), lambda l, i, o=start_b // n_b: (l, o, i))],
        out_specs=pl.BlockSpec((None, tk, LANES), lambda l, i: (l, i, 0)),
        out_shape=jax.ShapeDtypeStruct((depth, k, LANES), BF16),
        compiler_params=_cparams(("parallel", "parallel")), name="cast_w_small")(wt, wt)


def _norm_kernel(x_ref, g_ref, o_ref):
    x = x_ref[...]
    y = x * lax.rsqrt(jnp.mean(x * x, axis=-1, keepdims=True) + EPS)
    o_ref[...] = (y * g_ref[...]).astype(o_ref.dtype)


def rmsnorm_rows(x, g):
    m, d = x.shape
    tm = min(512, m)
    return pl.pallas_call(
        _norm_kernel, grid=(m // tm,),
        in_specs=[pl.BlockSpec((tm, d), lambda i: (i, 0)), pl.BlockSpec((1, d), lambda i: (0, 0))],
        out_specs=pl.BlockSpec((tm, d), lambda i: (i, 0)),
        out_shape=jax.ShapeDtypeStruct((m, d), BF16),
        compiler_params=_cparams(("parallel",)), name="rmsnorm")(x, g.reshape(1, d))


def _mm_kernel(*refs, n_lhs, ws, n_extra, n_const, epi, rider, w_valid_cols):
    n_pre = len(rider.prefetch) if rider else 0
    n_rin = len(rider.args) if rider else 0
    pre, refs = refs[:n_pre], refs[n_pre:]
    n_in = 2 * n_lhs + len(ws) + sum(has_bias for _, has_bias in ws) + 2 * n_extra + n_const
    it = iter(refs[:n_in])
    lhs_big = [next(it) for _ in range(n_lhs)]
    lhs_small = [next(it) for _ in range(n_lhs)]
    w_refs = [next(it) for _ in ws]
    b_refs = [next(it) if has_bias else None for (_, has_bias) in ws]
    ex_big = [next(it) for _ in range(n_extra)]
    ex_small = [next(it) for _ in range(n_extra)]
    consts = [next(it) for _ in range(n_const)]
    rider_in = refs[n_in:n_in + n_rin]
    out_big, out_small = refs[n_in + n_rin:n_in + n_rin + 2]
    rider_rest = refs[n_in + n_rin + 2:]

    def run(lhs, extras, out):
        accs = []
        for k, (li, _) in enumerate(ws):
            a = jnp.dot(lhs[li][...], w_refs[k][...], preferred_element_type=F32)
            if b_refs[k] is not None:
                a = a + b_refs[k][...]
            accs.append(a)
        out[...] = epi(accs, [e[...] for e in extras], [c[...] for c in consts]).astype(out.dtype)

    rider_refs = (*pre, *rider_in, *rider_rest)
    if w_valid_cols is not None:
        w_f32, w_refs = w_refs, rider_rest

        @pl.when(pl.program_id(1) == 0)
        def _():
            for src, dst in zip(w_f32, w_refs):
                w = src[...]
                cols = pl.program_id(0) * w.shape[1] + lax.broadcasted_iota(jnp.int32, w.shape, 1)
                dst[...] = jnp.where(cols < w_valid_cols, w, 0.0).astype(dst.dtype)
    if rider:
        rider.phases[0](*rider_refs)
    run(lhs_big, ex_big, out_big)
    if rider:
        rider.phases[1](*rider_refs)
        rider.phases[2](*rider_refs)

    @pl.when(pl.program_id(1) == 0)
    def _():
        run(lhs_small, ex_small, out_small)


class Rider:
    def __init__(self, grid, prefetch, in_specs, args, out_specs, out_shape, scratch_shapes, phases):
        self.grid, self.prefetch, self.in_specs, self.args = grid, prefetch, in_specs, args
        self.out_specs, self.out_shape, self.scratch_shapes, self.phases = out_specs, out_shape, scratch_shapes, phases


def _cast_rows_kernel(x_ref, o_ref, *, grid_rows, valid_rows):
    x = x_ref[...]
    if valid_rows is not None:
        step = pl.program_id(0) * grid_rows + pl.program_id(1)
        rows = step * x.shape[0] + lax.broadcasted_iota(jnp.int32, x.shape, 0)
        x = jnp.where(rows < valid_rows, x, 0.0)
    o_ref[...] = x.astype(o_ref.dtype)


def cast_rider(w, layer, grid, rows_out):
    _, rows, cols = w.shape
    steps = grid[0] * grid[1]
    tr = rows_out // steps
    assert rows_out % steps == 0 and tr % 16 == 0
    main = functools.partial(_cast_rows_kernel, grid_rows=grid[1], valid_rows=rows if rows_out != rows else None)
    nop = lambda *refs: None
    last = (rows - 1) // tr
    return Rider(
        grid=grid, prefetch=[],
        in_specs=[pl.BlockSpec((None, tr, cols), lambda j, i: (layer, jnp.minimum(j * grid[1] + i, last), 0))],
        args=[w],
        out_specs=[pl.BlockSpec((None, tr, cols), lambda j, i: (0, j * grid[1] + i, 0))],
        out_shape=[jax.ShapeDtypeStruct((1, rows_out, cols), BF16)], scratch_shapes=[], phases=(nop, main, nop))


def fused_matmul(lhs, layer, weights, extras, consts, epi, n_cols, out_dtype, tm, tn, name, single_buffer_w=False,
                 rider=None, cast_weights_in_kernel=False):
    r = lhs[0][0].shape[0]
    s = lhs[0][1].shape[0]
    tm = min(tm, r)
    assert r % tm == 0 and n_cols % tn == 0
    grid = (n_cols // tn, r // tm)
    assert rider is None or rider.grid == grid

    def spec(shape, index, **kw):
        return pl.BlockSpec(shape, lambda j, i, *_: index(j, i), **kw)

    in_specs, args = [], []
    for big, _ in lhs:
        in_specs.append(spec((tm, big.shape[1]), lambda j, i: (i, 0)))
        args.append(big)
    for _, small in lhs:
        in_specs.append(spec(small.shape, lambda j, i: (0, 0)))
        args.append(small)
    w_mode = dict(pipeline_mode=pl.Buffered(1)) if single_buffer_w else {}
    for w, _, off, _ in weights:
        assert off % tn == 0
        in_specs.append(spec((None, w.shape[1], tn), functools.partial(lambda o, j, i: (layer, 0, j + o), off // tn),
                             **w_mode))
        args.append(w)
    for _, _, _, b in weights:
        if b is not None:
            assert b.shape == (1, n_cols)
            in_specs.append(spec((1, tn), lambda j, i: (0, j)))
            args.append(b)
    for big, _, off in extras:
        assert off % tn == 0
        in_specs.append(spec((tm, tn), functools.partial(lambda o, j, i: (i, j + o), off // tn)))
        args.append(big)
    for _, small, off in extras:
        in_specs.append(spec((s, tn), functools.partial(lambda o, j, i: (0, j + o), off // tn)))
        args.append(small)
    for c in consts:
        if c.shape == (1, n_cols):
            in_specs.append(spec((1, tn), lambda j, i: (0, j)))
        else:
            in_specs.append(spec(c.shape, functools.partial(lambda nd, j, i: (0,) * nd, c.ndim)))
        args.append(c)
    out_specs = [spec((tm, tn), lambda j, i: (i, j)), spec((s, tn), lambda j, i: (0, j))]
    out_shape = [jax.ShapeDtypeStruct((r, n_cols), out_dtype), jax.ShapeDtypeStruct((s, n_cols), out_dtype)]
    assert not (rider and cast_weights_in_kernel)
    kern = functools.partial(
        _mm_kernel, n_lhs=len(lhs), ws=[(li, b is not None) for _, li, _, b in weights],
        n_extra=len(extras), n_const=len(consts), epi=epi, rider=rider,
        w_valid_cols=weights[0][0].shape[2] if cast_weights_in_kernel else None)
    if rider is None:
        scratch = [pltpu.VMEM((w.shape[1], tn), BF16) for w, _, _, _ in weights] if cast_weights_in_kernel else []
        return pl.pallas_call(kern, grid=grid, in_specs=in_specs, out_specs=out_specs, out_shape=out_shape,
                              scratch_shapes=scratch,
                              compiler_params=_cparams(("parallel", "arbitrary")), name=name)(*args)
    grid_spec = pltpu.PrefetchScalarGridSpec(
        num_scalar_prefetch=len(rider.prefetch), grid=grid, in_specs=in_specs + rider.in_specs,
        out_specs=out_specs + rider.out_specs, scratch_shapes=rider.scratch_shapes)
    return pl.pallas_call(kern, grid_spec=grid_spec, out_shape=out_shape + rider.out_shape,
                          compiler_params=_cparams(("arbitrary", "arbitrary")), name=name)(
                              *rider.prefetch, *args, *rider.args)


def _epi_scale(accs, extras, consts):
    return accs[0] * consts[0]


def _epi_identity(accs, extras, consts):
    return accs[0]


def _epi_headnorm(accs, extras, consts):
    y, g = accs[0], consts[0]
    outs = []
    for h in range(y.shape[1] // LANES):
        yh = y[:, h * LANES:(h + 1) * LANES]
        outs.append(yh * lax.rsqrt(jnp.mean(yh * yh, axis=-1, keepdims=True) + EPS) * g)
    return jnp.concatenate(outs, axis=-1)


def _epi_gated2(accs, extras, consts):
    return (jax.nn.sigmoid(extras[0].astype(F32)) * accs[0] + jax.nn.sigmoid(extras[1].astype(F32)) * accs[1])


def _epi_residual(accs, extras, consts):
    return extras[0] + accs[0]


def _epi_swiglu(accs, extras, consts):
    return jax.nn.silu(accs[0]) * accs[1]


def _gates_kernel(h_ref, w_ref, b_ref, c0_ref, l_ref, cb_ref, cf_ref, lt_ref, cbt_ref, cft_ref, carry,
                  *, n_lin, n_gate, cs):
    @pl.when(pl.program_id(1) == 0)
    def _():
        carry[...] = c0_ref[...]

    z = jnp.dot(h_ref[...], w_ref[...], preferred_element_type=F32) + b_ref[...]
    lane = lax.broadcasted_iota(jnp.int32, z.shape, 1)
    logs = jnp.where(lane < n_lin, z, jnp.where(lane < n_gate, jax.nn.log_sigmoid(z), 0.0))
    tri = (lax.broadcasted_iota(jnp.int32, (cs, cs), 1) <= lax.broadcasted_iota(jnp.int32, (cs, cs), 0)).astype(F32)
    for s in range(z.shape[0] // cs):
        rows = slice(s * cs, (s + 1) * cs)
        lc = logs[rows]
        cb = jnp.dot(tri, lc, precision=lax.Precision.HIGHEST, preferred_element_type=F32)
        cf = cb + carry[...]
        carry[...] = cf[cs - 1:cs, :]
        l_ref[rows, :] = lc
        cb_ref[rows, :] = cb
        cf_ref[rows, :] = cf
        lt_ref[:, rows] = lc.T
        cbt_ref[:, rows] = cb.T
        cft_ref[:, rows] = cf.T


def gate_logs(h, w, layer, bias, carry0, n_seq, n_lin, n_gate):
    r, d = h.shape
    seq = r // n_seq
    tile = min(512, seq)
    nt = seq // tile
    cs = min(CHUNK, tile)
    row = pl.BlockSpec((tile, LANES), lambda b, t: (b * nt + t, 0))
    col = pl.BlockSpec((LANES, tile), lambda b, t: (0, b * nt + t))
    kern = functools.partial(_gates_kernel, n_lin=n_lin, n_gate=n_gate, cs=cs)
    return pl.pallas_call(
        kern, grid=(n_seq, nt),
        in_specs=[pl.BlockSpec((tile, d), lambda b, t: (b * nt + t, 0)),
                  pl.BlockSpec((None, d, LANES), lambda b, t: (layer, 0, 0)),
                  pl.BlockSpec((1, LANES), lambda b, t: (0, 0)),
                  pl.BlockSpec((1, LANES), lambda b, t: (0, 0))],
        out_specs=[row, row, row, col, col, col],
        out_shape=[jax.ShapeDtypeStruct((r, LANES), F32)] * 3 + [jax.ShapeDtypeStruct((LANES, r), F32)] * 3,
        scratch_shapes=[pltpu.VMEM((1, LANES), F32)],
        compiler_params=_cparams(("arbitrary", "arbitrary")), name="gate_logs")(h, w, bias, carry0)


def _cast_block(x_ref, o_ref):
    x = x_ref[...].astype(o_ref.dtype)
    pad = o_ref.shape[1] - x.shape[1]
    if pad:
        x = jnp.concatenate([x, jnp.zeros((x.shape[0], pad), o_ref.dtype)], axis=1)
    o_ref[...] = x


def cast_block_specs(w, layer, cols_out, grid):
    _, rows, cols = w.shape
    steps = grid[0] * grid[1]
    tr = rows // steps
    assert rows % steps == 0 and tr % 16 == 0 and cols % LANES == 0 and cols_out % LANES == 0
    return (pl.BlockSpec((None, tr, cols), lambda a, b: (layer, a * grid[1] + b, 0)),
            pl.BlockSpec((None, tr, cols_out), lambda a, b: (0, a * grid[1] + b, 0)),
            jax.ShapeDtypeStruct((1, rows, cols_out), BF16))


def _mlstm_head(q, k, v, bcol, brow, icol, irow, c0, n0, m0, nv):
    ln = q.shape[0]
    g = bcol + m0
    d = bcol - brow + irow
    ti = lax.broadcasted_iota(jnp.int32, (ln, ln), 0)
    si = lax.broadcasted_iota(jnp.int32, (ln, ln), 1)
    d = jnp.where(si <= ti, d, -jnp.inf)
    m = jnp.maximum(g, jnp.max(d, axis=1, keepdims=True))
    w = jnp.exp(d - m)
    inter = jnp.exp(g - m)
    s = lax.dot_general(q, k, NT_DIMS, preferred_element_type=F32) * w
    num = (jnp.dot(s.astype(BF16), v, preferred_element_type=F32)
           + inter * jnp.dot(q, c0.astype(BF16), preferred_element_type=F32))
    den = jnp.sum(s, axis=1, keepdims=True) + inter * jnp.sum(q.astype(F32) * n0, axis=1, keepdims=True)
    h = num / jnp.maximum(jnp.abs(den), jnp.exp(-m))
    m_end = m[nv - 1:nv]
    sidx = lax.broadcasted_iota(jnp.int32, (ln, 1), 0)
    w_end = jnp.where(sidx < nv, jnp.exp(bcol[nv - 1:nv] - bcol + icol - m_end), 0.0)
    decay = jnp.exp(g[nv - 1:nv] - m_end)
    kw = k.astype(F32) * w_end
    c1 = decay * c0 + jnp.dot(kw.T.astype(BF16), v, preferred_element_type=F32)
    n1 = decay * n0 + jnp.sum(kw, axis=0, keepdims=True)
    return h, c1, n1, m_end


def _mlstm_kernel(q_ref, k_ref, v_ref, og_ref, g_ref, l_ref, cb_ref, lt_ref, cbt_ref, ci_ref, ni_ref, mi_ref,
                  *refs, n_heads, dqk, dv, nv, with_cast):
    if with_cast:
        w_ref, hm_ref, co_ref, no_ref, mo_ref, wo_ref, c_s, n_s, m_s = refs
    else:
        hm_ref, co_ref, no_ref, mo_ref, c_s, n_s, m_s = refs
    c = pl.program_id(1)

    @pl.when(c == 0)
    def _():
        c_s[...] = ci_ref[...]
        n_s[...] = ni_ref[...]
        m_s[...] = mi_ref[...]

    logs, cb, logs_t, cb_t = l_ref[...], cb_ref[...], lt_ref[...], cbt_ref[...]
    for h in range(n_heads):
        hh, c1, n1, m_end = _mlstm_head(
            q_ref[:, h * dqk:(h + 1) * dqk], k_ref[:, h * dqk:(h + 1) * dqk], v_ref[:, h * dv:(h + 1) * dv],
            cb[:, n_heads + h:n_heads + h + 1], cb_t[n_heads + h:n_heads + h + 1, :],
            logs[:, h:h + 1], logs_t[h:h + 1, :], c_s[h], n_s[h], m_s[h][:, 0:1], nv)
        c_s[h] = c1
        n_s[h] = n1
        m_s[h] = jnp.broadcast_to(m_end, (1, LANES))
        cols = slice(h * dv, (h + 1) * dv)
        hn = hh * lax.rsqrt(jnp.mean(hh * hh, axis=-1, keepdims=True) + EPS) * g_ref[:, cols]
        hm_ref[:, cols] = (hn * jax.nn.sigmoid(og_ref[:, cols].astype(F32))).astype(hm_ref.dtype)
    if with_cast:
        _cast_block(w_ref, wo_ref)

    @pl.when(c == pl.num_programs(1) - 1)
    def _():
        co_ref[0] = c_s[...]
        no_ref[0] = n_s[...]
        mo_ref[0] = m_s[...]


def mlstm_chunks(qkv, og, mh_g, logs, cb, logs_t, cb_t, c_init, n_init, m_init, n_seq, nv, cast=None):
    r = qkv.shape[0]
    n_heads, dqk, dv = c_init.shape
    nc = r // n_seq // CHUNK
    hq, hv = n_heads * dqk, n_heads * dv
    assert hv == 2 * hq
    rows = lambda b, c: b * nc + c
    kern = functools.partial(_mlstm_kernel, n_heads=n_heads, dqk=dqk, dv=dv, nv=nv, with_cast=cast is not None)
    full3 = lambda shape: pl.BlockSpec(shape, lambda b, c: (0, 0, 0))
    cast_in, cast_out, cast_shape = cast_block_specs(*cast, (n_seq, nc)) if cast else (None, None, None)
    return pl.pallas_call(
        kern, grid=(n_seq, nc),
        in_specs=[pl.BlockSpec((CHUNK, hq), lambda b, c: (rows(b, c), 0)),
                  pl.BlockSpec((CHUNK, hq), lambda b, c: (rows(b, c), 1)),
                  pl.BlockSpec((CHUNK, hv), lambda b, c: (rows(b, c), 1)),
                  pl.BlockSpec((CHUNK, hv), lambda b, c: (rows(b, c), 0)),
                  pl.BlockSpec((1, hv), lambda b, c: (0, 0)),
                  pl.BlockSpec((CHUNK, LANES), lambda b, c: (rows(b, c), 0)),
                  pl.BlockSpec((CHUNK, LANES), lambda b, c: (rows(b, c), 0)),
                  pl.BlockSpec((LANES, CHUNK), lambda b, c: (0, rows(b, c))),
                  pl.BlockSpec((LANES, CHUNK), lambda b, c: (0, rows(b, c))),
                  full3(c_init.shape), full3(n_init.shape), full3(m_init.shape)] + ([cast_in] if cast else []),
        out_specs=[pl.BlockSpec((CHUNK, hv), lambda b, c: (rows(b, c), 0)),
                   pl.BlockSpec((1,) + c_init.shape, lambda b, c: (b, 0, 0, 0)),
                   pl.BlockSpec((1,) + n_init.shape, lambda b, c: (b, 0, 0, 0)),
                   pl.BlockSpec((1,) + m_init.shape, lambda b, c: (b, 0, 0, 0))] + ([cast_out] if cast else []),
        out_shape=[jax.ShapeDtypeStruct((r, hv), BF16),
                   jax.ShapeDtypeStruct((n_seq,) + c_init.shape, F32),
                   jax.ShapeDtypeStruct((n_seq,) + n_init.shape, F32),
                   jax.ShapeDtypeStruct((n_seq,) + m_init.shape, F32)] + ([cast_shape] if cast else []),
        scratch_shapes=[pltpu.VMEM(c_init.shape, F32), pltpu.VMEM(n_init.shape, F32), pltpu.VMEM(m_init.shape, F32)],
        compiler_params=_cparams(("arbitrary", "arbitrary")), name="mlstm_chunks")(
            qkv, qkv, qkv, og, mh_g, logs, cb, logs_t, cb_t, c_init, n_init, m_init, *([cast[0]] if cast else []))


def _mlstm_step_kernel(q_ref, k_ref, v_ref, og_ref, g_ref, li_ref, lf_ref, c_ref, n_ref, m_ref,
                       hm_ref, co_ref, no_ref, mo_ref, *, n_heads):
    q, k = q_ref[0].astype(F32), k_ref[0].astype(F32)
    v = v_ref[0].astype(F32)
    og = og_ref[0].astype(F32)
    pad = jnp.zeros((LANES - n_heads, q.shape[1]), F32)
    q_t = jnp.concatenate([q, pad], axis=0).T
    k_t = jnp.concatenate([k, pad], axis=0).T
    logi, logf, m0 = li_ref[0], lf_ref[0], m_ref[0]
    g = logf + m0
    m = jnp.maximum(g, logi)
    w = jnp.exp(logi - m)
    inter = jnp.exp(g - m)
    floor = jnp.exp(-m)
    for h in range(n_heads):
        c0, n0 = c_ref[0, h], n_ref[0, h]
        qh, kh, vh = q[h:h + 1], k[h:h + 1], v[h:h + 1]
        wh, ih = w[h:h + 1], inter[h:h + 1]
        s = jnp.sum(qh * kh, axis=1, keepdims=True) * wh
        num = s * vh + ih * jnp.sum(q_t[:, h:h + 1] * c0, axis=0, keepdims=True)
        den = s + ih * jnp.sum(qh * n0, axis=1, keepdims=True)
        hh = num / jnp.maximum(jnp.abs(den), floor[h:h + 1])
        co_ref[0, h] = ih * c0 + (wh * k_t[:, h:h + 1]) * vh
        no_ref[0, h] = ih * n0 + wh * kh
        hn = hh * lax.rsqrt(jnp.mean(hh * hh, axis=-1, keepdims=True) + EPS) * g_ref[h:h + 1]
        hm_ref[0, h:h + 1] = hn * jax.nn.sigmoid(og[h:h + 1])
    mo_ref[0] = m


def mlstm_step(q, k, v, og, mh_g, logi, logf, c0, n0, m0):
    nb, n_heads, dqk = q.shape
    dv = v.shape[2]
    b3 = lambda a: pl.BlockSpec((1,) + a.shape[1:], lambda i: (i,) + (0,) * (a.ndim - 1))
    outs = [jax.ShapeDtypeStruct((nb, n_heads, dv), F32), jax.ShapeDtypeStruct(c0.shape, F32),
            jax.ShapeDtypeStruct(n0.shape, F32), jax.ShapeDtypeStruct(m0.shape, F32)]
    return pl.pallas_call(
        functools.partial(_mlstm_step_kernel, n_heads=n_heads), grid=(nb,),
        in_specs=[b3(q), b3(k), b3(v), b3(og), pl.BlockSpec(mh_g.shape, lambda i: (0, 0)),
                  b3(logi), b3(logf), b3(c0), b3(n0), b3(m0)],
        out_specs=[b3(o) for o in outs], out_shape=outs,
        compiler_params=_cparams(("parallel",)), name="mlstm_step")(q, k, v, og, mh_g, logi, logf, c0, n0, m0)


def _fox_kernel(q_ref, k_ref, v_ref, km_ref, vm_ref, cf_ref, cft_ref, cmt_ref, *refs,
                tq, tk, n_meta, scale, lane0, causal_meta, with_cast):
    if with_cast:
        w_ref, o_ref, wo_ref, kb, vb = refs
        _cast_block(w_ref, wo_ref)
    else:
        o_ref, kb, vb = refs
    h = pl.program_id(1)
    seq = q_ref.shape[0]
    if not causal_meta:
        kb[...] = k_ref[...].astype(BF16)
        vb[...] = v_ref[...].astype(BF16)
    km = km_ref[...].astype(BF16)
    vm = vm_ref[...].astype(BF16)
    lane = lax.broadcasted_iota(jnp.int32, (tq, LANES), 1)
    row_id = lax.broadcasted_iota(jnp.int32, (tq, tk), 0)
    col_id = lax.broadcasted_iota(jnp.int32, (tq, tk), 1)

    scale2 = scale * LOG2E

    def online(carry, s, vals):
        m, l, acc = carry
        mn = jnp.maximum(m, jnp.max(s, axis=1, keepdims=True))
        alpha = jnp.exp2(m - mn)
        p = jnp.exp2(s - mn)
        return (mn, alpha * l + jnp.sum(p, axis=1, keepdims=True),
                alpha * acc + jnp.dot(p.astype(BF16), vals, preferred_element_type=F32))

    for i in range(seq // tq):
        rows = slice(i * tq, (i + 1) * tq)
        q = q_ref[rows, :]
        fq = jnp.sum(jnp.where(lane == lane0 + h, cf_ref[rows, :], 0.0), axis=1, keepdims=True) * LOG2E
        s = lax.dot_general(q, km, NT_DIMS, preferred_element_type=F32) * scale2 + fq - cmt_ref[...] * LOG2E
        col = lax.broadcasted_iota(jnp.int32, s.shape, 1)
        ok = col < n_meta
        if causal_meta:
            ok = ok & (col <= lax.broadcasted_iota(jnp.int32, s.shape, 0))
        s = jnp.where(ok, s, -jnp.inf)
        m = jnp.max(s, axis=1, keepdims=True)
        p = jnp.exp2(s - m)
        carry = (m, jnp.sum(p, axis=1, keepdims=True), jnp.dot(p.astype(BF16), vm, preferred_element_type=F32))
        if not causal_meta:
            for j in range(-(-(i + 1) * tq // tk)):
                cols = slice(j * tk, (j + 1) * tk)
                s = (lax.dot_general(q, kb[cols, :], NT_DIMS, preferred_element_type=F32) * scale2
                     + fq - cft_ref[j:j + 1, :] * LOG2E)
                if (j + 1) * tk - 1 > i * tq:
                    s = jnp.where(col_id + (j * tk - i * tq) <= row_id, s, -jnp.inf)
                carry = online(carry, s, vb[cols, :])
        o_ref[rows, :] = (carry[2] / carry[1]).astype(o_ref.dtype)


def fox_prompt(q, k, v, k_meta, v_meta, cf, cft, cmt, n_seq, n_meta, lane0, scale, causal_meta, cast=None):
    r, width = q.shape
    dh = LANES
    n_heads = width // dh
    seq = r // n_seq
    tq, tk = min(FOX_TQ, seq), min(FOX_TK, seq)
    s_rows = k_meta.shape[0]
    cft3 = cft.reshape(LANES, n_seq, seq // tk, tk)
    cmt3 = cmt.reshape(LANES, 1, s_rows)
    kern = functools.partial(_fox_kernel, tq=tq, tk=tk, n_meta=n_meta, scale=scale, lane0=lane0,
                             causal_meta=causal_meta, with_cast=cast is not None)
    blk = pl.BlockSpec((seq, dh), lambda b, h: (b, h))
    cast_in, cast_out, cast_shape = cast_block_specs(*cast, (n_seq, n_heads)) if cast else (None, None, None)
    out_shape = jax.ShapeDtypeStruct((r, width), BF16)
    return pl.pallas_call(
        kern, grid=(n_seq, n_heads),
        in_specs=[blk, blk, blk,
                  pl.BlockSpec((s_rows, dh), lambda b, h: (0, h)),
                  pl.BlockSpec((s_rows, dh), lambda b, h: (0, h)),
                  pl.BlockSpec((seq, LANES), lambda b, h: (b, 0)),
                  pl.BlockSpec((None, None, seq // tk, tk), lambda b, h: (lane0 + h, b, 0, 0)),
                  pl.BlockSpec((None, 1, s_rows), lambda b, h: (lane0 + h, 0, 0))] + ([cast_in] if cast else []),
        out_specs=[blk, cast_out] if cast else blk,
        out_shape=[out_shape, cast_shape] if cast else out_shape,
        scratch_shapes=[pltpu.VMEM((seq, dh), BF16), pltpu.VMEM((seq, dh), BF16)],
        compiler_params=_cparams(("parallel", "arbitrary")), name="fox_prompt")(
            q, k, v, k_meta, v_meta, cf, cft3, cmt3, *([cast[0]] if cast else []))


def _page_logs_kernel(x_ref, r_ref, t_ref, *, n_heads, rpp, sub_pages):
    hi = lax.Precision.HIGHEST
    li = lax.broadcasted_iota(jnp.int32, (LANES, LANES), 0)
    lo = lax.broadcasted_iota(jnp.int32, (LANES, LANES), 1)
    same_head = (li % n_heads) == (lo % n_heads)
    w_row = (same_head & (li // n_heads >= lo // n_heads)).astype(F32)
    w_all = same_head.astype(F32)
    sb = sub_pages * rpp
    ri = lax.broadcasted_iota(jnp.int32, (sb, sb), 0)
    ro = lax.broadcasted_iota(jnp.int32, (sb, sb), 1)
    later_rows = ((ro > ri) & (ro // rpp == ri // rpp)).astype(F32)
    pi = lax.broadcasted_iota(jnp.int32, (sub_pages * 8, sb), 0)
    po = lax.broadcasted_iota(jnp.int32, (sub_pages * 8, sb), 1)
    page_rows = (po // rpp == pi // 8).astype(F32)
    for s in range(x_ref.shape[0] // sb):
        x = x_ref[s * sb:(s + 1) * sb, :]
        row_tot = jnp.dot(x, w_all, precision=hi, preferred_element_type=F32)
        r_ref[s * sb:(s + 1) * sb, :] = (jnp.dot(x, w_row, precision=hi, preferred_element_type=F32)
                                         + jnp.dot(later_rows, row_tot, precision=hi, preferred_element_type=F32) - x)
        t_ref[s * sub_pages * 8:(s + 1) * sub_pages * 8, :] = jnp.dot(page_rows, row_tot, precision=hi,
                                                                      preferred_element_type=F32)


def page_logs(cache_lf):
    depth, n_pool, page, n_heads = cache_lf.shape
    assert LANES % n_heads == 0 and (page * n_heads) % (8 * LANES) == 0
    rpp = page * n_heads // LANES
    n_pages = depth * n_pool
    sub_pages = _largest_divisor(n_pages, max(1, LANES // rpp))
    tile_pages = sub_pages * _largest_divisor(n_pages // sub_pages, 8)
    x = cache_lf.reshape(n_pages * rpp, LANES)
    kern = functools.partial(_page_logs_kernel, n_heads=n_heads, rpp=rpp, sub_pages=sub_pages)
    r, t = pl.pallas_call(
        kern, grid=(n_pages // tile_pages,),
        in_specs=[pl.BlockSpec((tile_pages * rpp, LANES), lambda i: (i, 0))],
        out_specs=[pl.BlockSpec((tile_pages * rpp, LANES), lambda i: (i, 0)),
                   pl.BlockSpec((tile_pages * 8, LANES), lambda i: (i, 0))],
        out_shape=[jax.ShapeDtypeStruct((n_pages * rpp, LANES), F32), jax.ShapeDtypeStruct((n_pages * 8, LANES), F32)],
        compiler_params=_cparams(("parallel",)), name="page_logs")(x)
    return r.reshape(depth, n_pool, rpp, LANES), t.reshape(depth, n_pool, 8, LANES)


def _decode_refs(refs, pages_per_step):
    q_ref, kn_ref, vn_ref, fq_ref = refs[1:5]
    page_refs = refs[5:5 + 4 * pages_per_step]
    o_ref, m_s, l_s, acc, carry = refs[5 + 4 * pages_per_step:]
    return q_ref, kn_ref, vn_ref, fq_ref, page_refs, o_ref, m_s, l_s, acc, carry


def _fox_decode_init(*refs, scale, pages_per_step):
    q_ref, kn_ref, vn_ref, _, _, _, m_s, l_s, acc, carry = _decode_refs(refs, pages_per_step)

    @pl.when(pl.program_id(1) == 0)
    def _():
        m_s[...] = jnp.sum(q_ref[...].astype(F32) * kn_ref[...], axis=1, keepdims=True) * scale
        l_s[...] = jnp.ones_like(l_s)
        acc[...] = vn_ref[...]
        carry[...] = jnp.zeros_like(carry)


def _fox_decode_pages(*refs, n_heads, scale, pages_per_step):
    q_ref, _, _, fq_ref, page_refs, _, m_s, l_s, acc, carry = _decode_refs(refs, pages_per_step)
    q = q_ref[...]
    fq = fq_ref[...]
    later_pages = carry[0:1, :]
    m, l, a = m_s[...], l_s[...], acc[...]
    for u in range(pages_per_step):
        k_ref, v_ref, r_ref, t_ref = page_refs[4 * u:4 * u + 4]
        later = r_ref[...] + later_pages
        later_pages = later_pages + t_ref[0:1, :]
        bias = jnp.concatenate([jnp.broadcast_to(later[c:c + 1], (n_heads, LANES)) for c in range(later.shape[0])],
                               axis=1)
        s = lax.dot_general(q, k_ref[...].astype(BF16), NT_DIMS, preferred_element_type=F32) * scale + fq + bias
        own = (lax.broadcasted_iota(jnp.int32, s.shape, 1) % n_heads) == lax.broadcasted_iota(jnp.int32, s.shape, 0)
        s = jnp.where(own, s, -jnp.inf)
        mn = jnp.maximum(m, jnp.max(s, axis=1, keepdims=True))
        alpha = jnp.exp(m - mn)
        pr = jnp.exp(s - mn)
        l = alpha * l + jnp.sum(pr, axis=1, keepdims=True)
        a = alpha * a + jnp.dot(pr.astype(BF16), v_ref[...].astype(BF16), preferred_element_type=F32)
        m = mn
    m_s[...], l_s[...], acc[...] = m, l, a
    carry[...] = jnp.broadcast_to(later_pages, carry.shape)


def _fox_decode_finish(*refs, pages_per_step):
    _, _, _, _, _, o_ref, _, l_s, acc, _ = _decode_refs(refs, pages_per_step)

    @pl.when(pl.program_id(1) == pl.num_programs(1) - 1)
    def _():
        o_ref[...] = acc[...] / l_s[...]


def fox_decode_job(page_table, layer, q, k_new, v_new, fq, cache_k, cache_v, later_logs, page_totals, scale,
                   steps_per_seq=None):
    nb, n_pages = page_table.shape
    depth, n_pool, page, n_heads, dh = cache_k.shape
    rows = page * n_heads
    ck = cache_k.reshape(depth, n_pool, rows, dh)
    cv = cache_v.reshape(depth, n_pool, rows, dh)
    pps = n_pages // steps_per_seq if steps_per_seq else _largest_divisor(n_pages, 8)
    assert n_pages % pps == 0
    phases = (functools.partial(_fox_decode_init, scale=scale, pages_per_step=pps),
              functools.partial(_fox_decode_pages, n_heads=n_heads, scale=scale, pages_per_step=pps),
              functools.partial(_fox_decode_finish, pages_per_step=pps))
    per_b = lambda w: pl.BlockSpec((None, n_heads, w), lambda i, p, pt: (i, 0, 0))
    paged = lambda a, u: pl.BlockSpec((None, None) + a.shape[2:],
                                      lambda i, p, pt: (layer, pt[i, n_pages - 1 - (p * pps + u)], 0, 0))
    page_specs, page_args = [], []
    for u in range(pps):
        for a in (ck, cv, later_logs, page_totals):
            page_specs.append(paged(a, u))
            page_args.append(a)
    return Rider(
        grid=(nb, n_pages // pps), prefetch=[page_table],
        in_specs=[per_b(dh), per_b(dh), per_b(dh), per_b(1)] + page_specs, args=[q, k_new, v_new, fq] + page_args,
        out_specs=[per_b(dh)], out_shape=[jax.ShapeDtypeStruct((nb, n_heads, dh), F32)],
        scratch_shapes=[pltpu.VMEM((n_heads, 1), F32), pltpu.VMEM((n_heads, 1), F32), pltpu.VMEM((n_heads, dh), F32),
                        pltpu.VMEM((8, LANES), F32)],
        phases=phases)


def run_alone(job, name):
    def body(*refs):
        for phase in job.phases:
            phase(*refs)

    grid_spec = pltpu.PrefetchScalarGridSpec(
        num_scalar_prefetch=len(job.prefetch), grid=job.grid, in_specs=job.in_specs, out_specs=job.out_specs,
        scratch_shapes=job.scratch_shapes)
    return pl.pallas_call(body, grid_spec=grid_spec, out_shape=job.out_shape,
                          compiler_params=_cparams(("arbitrary",) * len(job.grid)), name=name)(
                              *job.prefetch, *job.args)


def kernel(x_prompt, x_sample, cache_k, cache_v, cache_lf, state_c, state_n, state_m, page_table, meta, norm1_g,
           w_in, b_in, q_norm_g, k_norm_g, mh_norm_g, p_a, p_b, w_out, norm2_g, w_gate, w_up, w_down):
    nb, seq, d = x_prompt.shape
    db = x_sample.shape[0]
    assert x_sample.shape[1] == 1
    n_meta = meta.shape[0]
    depth = w_in.shape[0]
    _, _, hm, dqk, dv = state_c.shape
    _, _, _, hf, dh = cache_k.shape
    d_ff = w_gate.shape[2]
    assert dh == LANES and dqk == LANES and seq % CHUNK == 0 and n_meta + db <= SMALL_ROWS
    assert 2 * hm + hf <= LANES
    r = nb * seq
    sizes = (hm * dqk, hm * dqk, hm * dv, hm * dv, hm, hm, hf * dh, hf * dh, hf * dh, hf, d, d)
    st = [0]
    for sz in sizes:
        st.append(st[-1] + sz)
    n_gate = 2 * hm + hf
    n_mqkv, n_mo, n_f = st[3], hm * dv, hf * dh
    d_ffp = d_ff + (-d_ff % 1024)
    scale_f = dh ** -0.5
    sample = slice(n_meta, n_meta + db)

    w_in_t = jnp.swapaxes(w_in, 1, 2)
    w_m = cast_weights_t(w_in_t, 0, st[4], "cast_w_m")
    w_f = cast_weights_t(w_in_t, st[6], 3 * n_f, "cast_w_f")
    w_g = cast_weights_t(w_in_t, st[10], 2 * d, "cast_w_g")
    w_small = gate_weights_t(w_in_t, st[4], 2 * hm, st[9], hf)

    xb = x_prompt.reshape(r, d)
    xs = jnp.concatenate([meta.astype(F32), x_sample.reshape(db, d),
                          jnp.zeros((SMALL_ROWS - n_meta - db, d), F32)], axis=0)
    qk_scale = jnp.concatenate([jnp.ones((1, hm * dqk), F32), jnp.full((1, hm * dqk), dqk ** -0.5, F32),
                                jnp.ones((1, hm * dv), F32)], axis=1)
    zero_row = jnp.zeros((1, LANES), F32)
    later_logs, page_totals = page_logs(cache_lf)
    outs = [[] for _ in range(12)]

    for l in range(depth):
        b_l = b_in[l].reshape(1, -1)
        b_small = jnp.concatenate([b_l[:, st[4]:st[6]], b_l[:, st[9]:st[10]], jnp.zeros((1, LANES - n_gate), F32)], axis=1)

        hb, hs = rmsnorm_rows(xb, norm1_g[l]), rmsnorm_rows(xs, norm1_g[l])
        lhs = [(hb, hs)]
        tm_in = min(1024, r)
        qkv, qkv_s, wo = fused_matmul(lhs, l, [(w_m, 0, 0, b_l[:, :n_mqkv])], [], [qk_scale], _epi_scale, n_mqkv, BF16,
                                      tm_in, 1024, "in_qkv",
                                      rider=cast_rider(w_out, l, (n_mqkv // 1024, r // tm_in), d))
        og, og_s, pa = fused_matmul(lhs, l, [(w_m, 0, n_mqkv, b_l[:, st[3]:st[4]])], [], [], _epi_identity, n_mo, BF16,
                                    tm_in, 1024, "in_mo", rider=cast_rider(p_a, l, (n_mo // 1024, r // tm_in), n_mo))
        fq, fq_s, pb = fused_matmul(lhs, l, [(w_f, 0, 0, b_l[:, st[6]:st[7]])], [], [q_norm_g[l].reshape(1, dh)],
                                    _epi_headnorm, n_f, BF16, tm_in, 1024, "in_fq",
                                    rider=cast_rider(p_b, l, (n_f // 1024, r // tm_in), n_f))
        fk, fk_s = fused_matmul(lhs, l, [(w_f, 0, n_f, b_l[:, st[7]:st[8]])], [], [k_norm_g[l].reshape(1, dh)],
                                _epi_headnorm, n_f, F32, 1024, 1024, "in_fk")
        fv, fv_s = fused_matmul(lhs, l, [(w_f, 0, 2 * n_f, b_l[:, st[8]:st[9]])], [], [], _epi_identity, n_f, F32,
                                1024, 1024, "in_fv")
        ls, cbs, _, lst, cbst, _ = gate_logs(hs, w_small, l, b_small, zero_row, 1, hm, n_gate)
        decode_in = (page_table, l, fq_s[sample].reshape(db, hf, dh), fk_s[sample].reshape(db, hf, dh),
                     fv_s[sample].reshape(db, hf, dh), ls[sample, 2 * hm:n_gate].reshape(db, hf, 1),
                     cache_k, cache_v, later_logs, page_totals, scale_f)
        page_steps = page_table.shape[1] // 4 if page_table.shape[1] % 4 == 0 else 0
        ride = 2 * d // 1024 == db and page_steps and r % page_steps == 0 and (r // page_steps) % LANES == 0
        if ride:
            gab, gab_s, hf_d = fused_matmul(lhs, l, [(w_g, 0, 0, b_l[:, st[10]:])], [], [], _epi_identity, 2 * d, BF16,
                                            r // page_steps, 1024, "in_gab_decode",
                                            rider=fox_decode_job(*decode_in, steps_per_seq=page_steps))
        else:
            gab, gab_s = fused_matmul(lhs, l, [(w_g, 0, 0, b_l[:, st[10]:])], [], [], _epi_identity, 2 * d, BF16,
                                      1024, 1024, "in_gab")
            hf_d, = run_alone(fox_decode_job(*decode_in), "fox_decode")
        lb, cbb, cfb, lbt, cbbt, cfbt = gate_logs(hb, w_small, l, b_small, cbs[n_meta - 1:n_meta], nb, hm, n_gate)

        mh_g = mh_norm_g[l].reshape(1, hm * dv)
        hm_meta, c_m, n_m, m_m = mlstm_chunks(
            qkv_s, og_s, mh_g, ls, cbs, lst, cbst, jnp.zeros((hm, dqk, dv), F32), jnp.zeros((hm, 1, dqk), F32),
            jnp.zeros((hm, 1, LANES), F32), 1, n_meta)
        hm_b, c_p, n_p, m_p, wg = mlstm_chunks(qkv, og, mh_g, lb, cbb, lbt, cbbt, c_m[0], n_m[0], m_m[0], nb, CHUNK,
                                               cast=(w_gate, l, d_ffp))
        hm_d, c_d, n_d, m_d = mlstm_step(
            qkv_s[sample, :hm * dqk].reshape(db, hm, dqk), qkv_s[sample, hm * dqk:2 * hm * dqk].reshape(db, hm, dqk),
            qkv_s[sample, 2 * hm * dqk:].reshape(db, hm, dv), og_s[sample].reshape(db, hm, dv),
            mh_g.reshape(hm, dv), ls[sample, :hm].reshape(db, hm, 1), ls[sample, hm:2 * hm].reshape(db, hm, 1),
            state_c[l], state_n[l].reshape(db, hm, 1, dqk), state_m[l].reshape(db, hm, 1))
        pad_rows = jnp.zeros((SMALL_ROWS - n_meta - db, hm * dv), BF16)
        hm_s = jnp.concatenate([hm_meta[:n_meta], hm_d.reshape(db, hm * dv).astype(BF16), pad_rows], axis=0)

        hf_meta = fox_prompt(fq_s, fk_s, fv_s, fk_s, fv_s, cbs, cbst, cbst, 1, n_meta, 2 * hm, scale_f, True)
        hf_b, wu = fox_prompt(fq, fk, fv, fk_s, fv_s, cfb, cfbt, cbst, nb, n_meta, 2 * hm, scale_f, False,
                              cast=(w_up, l, d_ffp))
        hf_s = jnp.concatenate([hf_meta[:n_meta], hf_d.reshape(db, hf * dh).astype(BF16), pad_rows], axis=0)

        tm_mg = min(512, r)
        u, u_s, wd = fused_matmul([(hm_b, hm_s), (hf_b, hf_s)], 0, [(pa, 0, 0, None), (pb, 1, 0, None)],
                                  [(gab, gab_s, 0), (gab, gab_s, d)], [], _epi_gated2, d, BF16, tm_mg, 1024,
                                  "merge_gate", rider=cast_rider(w_down, l, (d // 1024, r // tm_mg), d_ffp))
        x1, x1_s = fused_matmul([(u, u_s)], 0, [(wo, 0, 0, None)], [(xb, xs, 0)], [], _epi_residual, d, F32,
                                512, 1024, "out_proj")
        h2, h2_s = rmsnorm_rows(x1, norm2_g[l]), rmsnorm_rows(x1_s, norm2_g[l])
        act, act_s = fused_matmul([(h2, h2_s)], 0, [(wg, 0, 0, None), (wu, 0, 0, None)], [], [], _epi_swiglu,
                                  d_ffp, BF16, 512, 1024, "ffn_up")
        xb, xs = fused_matmul([(act, act_s)], 0, [(wd, 0, 0, None)], [(x1, x1_s, 0)], [], _epi_residual, d, F32,
                              256, 1024, "ffn_down", single_buffer_w=True)

        outs[0].append((fk_s.reshape(SMALL_ROWS, hf, dh), fk.reshape(r, hf, dh)))
        outs[1].append((fv_s.reshape(SMALL_ROWS, hf, dh), fv.reshape(r, hf, dh)))
        outs[2].append((ls[:, 2 * hm:n_gate], lb[:, 2 * hm:n_gate]))
        outs[3].append(c_p)
        outs[4].append(n_p.reshape(nb, hm, dqk))
        outs[5].append(m_p[:, :, 0, 0])
        outs[6].append(fk_s[sample].reshape(db, 1, hf, dh))
        outs[7].append(fv_s[sample].reshape(db, 1, hf, dh))
        outs[8].append(ls[sample, 2 * hm:n_gate].reshape(db, 1, hf))
        outs[9].append(c_d)
        outs[10].append(n_d.reshape(db, hm, dqk))
        outs[11].append(m_d.reshape(db, hm))

    def with_meta(pairs):
        pieces = [p for small, big in pairs for b in range(nb) for p in (small[:n_meta], big[b * seq:(b + 1) * seq])]
        return jnp.concatenate(pieces, axis=0).reshape((depth, nb, n_meta + seq) + pieces[0].shape[1:])

    return ((xb.reshape(nb, seq, d), xs[sample].reshape(db, 1, d)) + tuple(with_meta(o) for o in outs[:3])
            + tuple(jnp.stack(o) for o in outs[3:]))
```

```python
import functools

import jax
import jax.numpy as jnp
from jax import lax
from jax.experimental import pallas as pl
from jax.experimental.pallas import tpu as pltpu

F32 = jnp.float32
BF16 = jnp.bfloat16
EPS = 1e-6
LANES = 128
CHUNK = 128
SMALL_ROWS = 128
FOX_TQ, FOX_TK = 512, 512
VMEM_LIMIT = 58 * 1024 * 1024
NT_DIMS = (((1,), (1,)), ((), ()))
LOG2E = 1.4426950408889634


def _cparams(sem):
    return pltpu.CompilerParams(dimension_semantics=sem, vmem_limit_bytes=VMEM_LIMIT)


def _largest_divisor(n, cap, multiple_of=1):
    return max(k for k in range(multiple_of, cap + 1, multiple_of) if n % k == 0)


def _cast_kernel(*refs, shift, valid_rows, valid_cols):
    if shift:
        a_ref, b_ref, o_ref = refs
        wide = jnp.concatenate([a_ref[...], b_ref[...]], axis=1)
        x = pltpu.roll(wide, wide.shape[1] - shift, 1)[:, :a_ref.shape[1]]
    else:
        a_ref, o_ref = refs
        x = a_ref[...]
    tr, tc = x.shape
    if valid_rows is not None:
        rows = pl.program_id(1) * tr + lax.broadcasted_iota(jnp.int32, x.shape, 0)
        x = jnp.where(rows < valid_rows, x, 0.0)
    if valid_cols is not None:
        cols = pl.program_id(2) * tc + lax.broadcasted_iota(jnp.int32, x.shape, 1)
        x = jnp.where(cols < valid_cols, x, 0.0)
    o_ref[...] = x.astype(o_ref.dtype)


def cast_weights(w, col_start, n_cols, rows_out, cols_out, name):
    depth, rows, cols = w.shape
    shift = col_start % LANES
    base = col_start - shift
    tr = _largest_divisor(rows_out, 1024, 8)
    tc = _largest_divisor(cols_out, 2816, LANES)
    assert base % tc == 0 and (not shift or cols_out == n_cols)
    in_specs = [pl.BlockSpec((None, tr, tc), lambda l, i, j, o=base // tc: (l, i, j + o))]
    args = [w]
    if shift:
        in_specs.append(pl.BlockSpec((None, tr, LANES), lambda l, i, j, o=base // LANES, s=tc // LANES: (l, i, o + (j + 1) * s)))
        args.append(w)
    kern = functools.partial(_cast_kernel, shift=shift, valid_rows=rows if rows_out != rows else None,
                             valid_cols=n_cols if cols_out != n_cols else None)
    return pl.pallas_call(
        kern, grid=(depth, rows_out // tr, cols_out // tc), in_specs=in_specs,
        out_specs=pl.BlockSpec((None, tr, tc), lambda l, i, j: (l, i, j)),
        out_shape=jax.ShapeDtypeStruct((depth, rows_out, cols_out), BF16),
        compiler_params=_cparams(("parallel", "parallel", "parallel")), name=name)(*args)


def _cast_t_kernel(*refs, shift):
    if shift:
        a_ref, b_ref, o_ref = refs
        x = jnp.concatenate([a_ref[shift:, :], b_ref[...]], axis=0)
    else:
        a_ref, o_ref = refs
        x = a_ref[...]
    o_ref[...] = x.astype(o_ref.dtype).T


def cast_weights_t(wt, row_start, n_rows, name):
    depth, _, k = wt.shape
    tk = _largest_divisor(k, 1024, LANES)
    tn = _largest_divisor(n_rows, 2048, LANES)
    shift = row_start % tn
    base = row_start - shift
    assert shift % 8 == 0 and (not shift or tn % shift == 0)
    in_specs = [pl.BlockSpec((None, tn, tk), lambda l, i, j, o=base // tn: (l, j + o, i))]
    args = [wt]
    if shift:
        in_specs.append(pl.BlockSpec((None, shift, tk),
                                     lambda l, i, j, o=base // shift, s=tn // shift: (l, o + (j + 1) * s, i)))
        args.append(wt)
    return pl.pallas_call(
        functools.partial(_cast_t_kernel, shift=shift), grid=(depth, k // tk, n_rows // tn), in_specs=in_specs,
        out_specs=pl.BlockSpec((None, tk, tn), lambda l, i, j: (l, i, j)),
        out_shape=jax.ShapeDtypeStruct((depth, k, n_rows), BF16),
        compiler_params=_cparams(("parallel", "parallel", "parallel")), name=name)(*args)


def _gate_rows_kernel(a_ref, b_ref, o_ref):
    pad = jnp.zeros((LANES - a_ref.shape[0] - b_ref.shape[0], a_ref.shape[1]), F32)
    o_ref[...] = jnp.concatenate([a_ref[...], b_ref[...], pad], axis=0).T.astype(o_ref.dtype)


def gate_weights_t(wt, start_a, n_a, start_b, n_b):
    depth, _, k = wt.shape
    assert n_a % 8 == 0 and n_b % 8 == 0 and start_a % n_a == 0 and start_b % n_b == 0
    tk = _largest_divisor(k, 2048, LANES)
    return pl.pallas_call(
        _gate_rows_kernel, grid=(depth, k // tk),
        in_specs=[pl.BlockSpec((None, n_a, tk), lambda l, i, o=start_a // n_a: (l, o, i)),
                  pl.BlockSpec((None, n_b, tk), lambda l, i, o=start_b // n_b: (l, o, i))],
        out_specs=pl.BlockSpec((None, tk, LANES), lambda l, i: (l, i, 0)),
        out_shape=jax.ShapeDtypeStruct((depth, k, LANES), BF16),
        compiler_params=_cparams(("parallel", "parallel")), name="cast_w_small")(wt, wt)


def _norm_kernel(x_ref, g_ref, o_ref):
    x = x_ref[...]
    y = x * lax.rsqrt(jnp.mean(x * x, axis=-1, keepdims=True) + EPS)
    o_ref[...] = (y * g_ref[...]).astype(o_ref.dtype)


def rmsnorm_rows(x, g):
    m, d = x.shape
    tm = min(512, m)
    return pl.pallas_call(
        _norm_kernel, grid=(m // tm,),
        in_specs=[pl.BlockSpec((tm, d), lambda i: (i, 0)), pl.BlockSpec((1, d), lambda i: (0, 0))],
        out_specs=pl.BlockSpec((tm, d), lambda i: (i, 0)),
        out_shape=jax.ShapeDtypeStruct((m, d), BF16),
        compiler_params=_cparams(("parallel",)), name="rmsnorm")(x, g.reshape(1, d))


def _mm_kernel(*refs, n_lhs, ws, n_extra, n_const, epi, rider, w_valid_cols):
    n_pre = len(rider.prefetch) if rider else 0
    n_rin = len(rider.args) if rider else 0
    pre, refs = refs[:n_pre], refs[n_pre:]
    n_in = 2 * n_lhs + len(ws) + sum(has_bias for _, has_bias in ws) + 2 * n_extra + n_const
    it = iter(refs[:n_in])
    lhs_big = [next(it) for _ in range(n_lhs)]
    lhs_small = [next(it) for _ in range(n_lhs)]
    w_refs = [next(it) for _ in ws]
    b_refs = [next(it) if has_bias else None for (_, has_bias) in ws]
    ex_big = [next(it) for _ in range(n_extra)]
    ex_small = [next(it) for _ in range(n_extra)]
    consts = [next(it) for _ in range(n_const)]
    rider_in = refs[n_in:n_in + n_rin]
    out_big, out_small = refs[n_in + n_rin:n_in + n_rin + 2]
    rider_rest = refs[n_in + n_rin + 2:]

    def run(lhs, extras, out):
        accs = []
        for k, (li, _) in enumerate(ws):
            a = jnp.dot(lhs[li][...], w_refs[k][...], preferred_element_type=F32)
            if b_refs[k] is not None:
                a = a + b_refs[k][...]
            accs.append(a)
        out[...] = epi(accs, [e[...] for e in extras], [c[...] for c in consts]).astype(out.dtype)

    rider_refs = (*pre, *rider_in, *rider_rest)
    if w_valid_cols is not None:
        w_f32, w_refs = w_refs, rider_rest

        @pl.when(pl.program_id(1) == 0)
        def _():
            for src, dst in zip(w_f32, w_refs):
                w = src[...]
                cols = pl.program_id(0) * w.shape[1] + lax.broadcasted_iota(jnp.int32, w.shape, 1)
                dst[...] = jnp.where(cols < w_valid_cols, w, 0.0).astype(dst.dtype)
    if rider:
        rider.phases[0](*rider_refs)
    run(lhs_big, ex_big, out_big)
    if rider:
        rider.phases[1](*rider_refs)
        rider.phases[2](*rider_refs)

    @pl.when(pl.program_id(1) == 0)
    def _():
        run(lhs_small, ex_small, out_small)


class Rider:
    def __init__(self, grid, prefetch, in_specs, args, out_specs, out_shape, scratch_shapes, phases):
        self.grid, self.prefetch, self.in_specs, self.args = grid, prefetch, in_specs, args
        self.out_specs, self.out_shape, self.scratch_shapes, self.phases = out_specs, out_shape, scratch_shapes, phases


def _cast_rows_kernel(x_ref, o_ref, *, grid_rows, valid_rows):
    x = x_ref[...]
    if valid_rows is not None:
        step = pl.program_id(0) * grid_rows + pl.program_id(1)
        rows = step * x.shape[0] + lax.broadcasted_iota(jnp.int32, x.shape, 0)
        x = jnp.where(rows < valid_rows, x, 0.0)
    o_ref[...] = x.astype(o_ref.dtype)


def cast_rider(w, layer, grid, rows_out):
    _, rows, cols = w.shape
    steps = grid[0] * grid[1]
    tr = rows_out // steps
    assert rows_out % steps == 0 and tr % 16 == 0
    main = functools.partial(_cast_rows_kernel, grid_rows=grid[1], valid_rows=rows if rows_out != rows else None)
    nop = lambda *refs: None
    last = (rows - 1) // tr
    return Rider(
        grid=grid, prefetch=[],
        in_specs=[pl.BlockSpec((None, tr, cols), lambda j, i: (layer, jnp.minimum(j * grid[1] + i, last), 0))],
        args=[w],
        out_specs=[pl.BlockSpec((None, tr, cols), lambda j, i: (0, j * grid[1] + i, 0))],
        out_shape=[jax.ShapeDtypeStruct((1, rows_out, cols), BF16)], scratch_shapes=[], phases=(nop, main, nop))


def fused_matmul(lhs, layer, weights, extras, consts, epi, n_cols, out_dtype, tm, tn, name, single_buffer_w=False,
                 rider=None, cast_weights_in_kernel=False):
    r = lhs[0][0].shape[0]
    s = lhs[0][1].shape[0]
    tm = min(tm, r)
    assert r % tm == 0 and n_cols % tn == 0
    grid = (n_cols // tn, r // tm)
    assert rider is None or rider.grid == grid

    def spec(shape, index, **kw):
        return pl.BlockSpec(shape, lambda j, i, *_: index(j, i), **kw)

    in_specs, args = [], []
    for big, _ in lhs:
        in_specs.append(spec((tm, big.shape[1]), lambda j, i: (i, 0)))
        args.append(big)
    for _, small in lhs:
        in_specs.append(spec(small.shape, lambda j, i: (0, 0)))
        args.append(small)
    w_mode = dict(pipeline_mode=pl.Buffered(1)) if single_buffer_w else {}
    for w, _, off, _ in weights:
        assert off % tn == 0
        in_specs.append(spec((None, w.shape[1], tn), functools.partial(lambda o, j, i: (layer, 0, j + o), off // tn),
                             **w_mode))
        args.append(w)
    for _, _, _, b in weights:
        if b is not None:
            assert b.shape == (1, n_cols)
            in_specs.append(spec((1, tn), lambda j, i: (0, j)))
            args.append(b)
    for big, _, off in extras:
        assert off % tn == 0
        in_specs.append(spec((tm, tn), functools.partial(lambda o, j, i: (i, j + o), off // tn)))
        args.append(big)
    for _, small, off in extras:
        in_specs.append(spec((s, tn), functools.partial(lambda o, j, i: (0, j + o), off // tn)))
        args.append(small)
    for c in consts:
        if c.shape == (1, n_cols):
            in_specs.append(spec((1, tn), lambda j, i: (0, j)))
        else:
            in_specs.append(spec(c.shape, functools.partial(lambda nd, j, i: (0,) * nd, c.ndim)))
        args.append(c)
    out_specs = [spec((tm, tn), lambda j, i: (i, j)), spec((s, tn), lambda j, i: (0, j))]
    out_shape = [jax.ShapeDtypeStruct((r, n_cols), out_dtype), jax.ShapeDtypeStruct((s, n_cols), out_dtype)]
    assert not (rider and cast_weights_in_kernel)
    kern = functools.partial(
        _mm_kernel, n_lhs=len(lhs), ws=[(li, b is not None) for _, li, _, b in weights],
        n_extra=len(extras), n_const=len(consts), epi=epi, rider=rider,
        w_valid_cols=weights[0][0].shape[2] if cast_weights_in_kernel else None)
    if rider is None:
        scratch = [pltpu.VMEM((w.shape[1], tn), BF16) for w, _, _, _ in weights] if cast_weights_in_kernel else []
        return pl.pallas_call(kern, grid=grid, in_specs=in_specs, out_specs=out_specs, out_shape=out_shape,
                              scratch_shapes=scratch,
                              compiler_params=_cparams(("parallel", "arbitrary")), name=name)(*args)
    grid_spec = pltpu.PrefetchScalarGridSpec(
        num_scalar_prefetch=len(rider.prefetch), grid=grid, in_specs=in_specs + rider.in_specs,
        out_specs=out_specs + rider.out_specs, scratch_shapes=rider.scratch_shapes)
    return pl.pallas_call(kern, grid_spec=grid_spec, out_shape=out_shape + rider.out_shape,
                          compiler_params=_cparams(("arbitrary", "arbitrary")), name=name)(
                              *rider.prefetch, *args, *rider.args)


def _epi_scale(accs, extras, consts):
    return accs[0] * consts[0]


def _epi_identity(accs, extras, consts):
    return accs[0]


def _epi_headnorm(accs, extras, consts):
    y, g = accs[0], consts[0]
    outs = []
    for h in range(y.shape[1] // LANES):
        yh = y[:, h * LANES:(h + 1) * LANES]
        outs.append(yh * lax.rsqrt(jnp.mean(yh * yh, axis=-1, keepdims=True) + EPS) * g)
    return jnp.concatenate(outs, axis=-1)


def _epi_gated2(accs, extras, consts):
    return (jax.nn.sigmoid(extras[0].astype(F32)) * accs[0] + jax.nn.sigmoid(extras[1].astype(F32)) * accs[1])


def _epi_residual(accs, extras, consts):
    return extras[0] + accs[0]


def _epi_swiglu(accs, extras, consts):
    return jax.nn.silu(accs[0]) * accs[1]


def _gates_kernel(h_ref, w_ref, b_ref, c0_ref, l_ref, cb_ref, cf_ref, lt_ref, cbt_ref, cft_ref, carry,
                  *, n_lin, n_gate, cs):
    @pl.when(pl.program_id(1) == 0)
    def _():
        carry[...] = c0_ref[...]

    z = jnp.dot(h_ref[...], w_ref[...], preferred_element_type=F32) + b_ref[...]
    lane = lax.broadcasted_iota(jnp.int32, z.shape, 1)
    logs = jnp.where(lane < n_lin, z, jnp.where(lane < n_gate, jax.nn.log_sigmoid(z), 0.0))
    tri = (lax.broadcasted_iota(jnp.int32, (cs, cs), 1) <= lax.broadcasted_iota(jnp.int32, (cs, cs), 0)).astype(F32)
    for s in range(z.shape[0] // cs):
        rows = slice(s * cs, (s + 1) * cs)
        lc = logs[rows]
        cb = jnp.dot(tri, lc, precision=lax.Precision.HIGHEST, preferred_element_type=F32)
        cf = cb + carry[...]
        carry[...] = cf[cs - 1:cs, :]
        l_ref[rows, :] = lc
        cb_ref[rows, :] = cb
        cf_ref[rows, :] = cf
        lt_ref[:, rows] = lc.T
        cbt_ref[:, rows] = cb.T
        cft_ref[:, rows] = cf.T


def gate_logs(h, w, layer, bias, carry0, n_seq, n_lin, n_gate):
    r, d = h.shape
    seq = r // n_seq
    tile = min(512, seq)
    nt = seq // tile
    cs = min(CHUNK, tile)
    row = pl.BlockSpec((tile, LANES), lambda b, t: (b * nt + t, 0))
    col = pl.BlockSpec((LANES, tile), lambda b, t: (0, b * nt + t))
    kern = functools.partial(_gates_kernel, n_lin=n_lin, n_gate=n_gate, cs=cs)
    return pl.pallas_call(
        kern, grid=(n_seq, nt),
        in_specs=[pl.BlockSpec((tile, d), lambda b, t: (b * nt + t, 0)),
                  pl.BlockSpec((None, d, LANES), lambda b, t: (layer, 0, 0)),
                  pl.BlockSpec((1, LANES), lambda b, t: (0, 0)),
                  pl.BlockSpec((1, LANES), lambda b, t: (0, 0))],
        out_specs=[row, row, row, col, col, col],
        out_shape=[jax.ShapeDtypeStruct((r, LANES), F32)] * 3 + [jax.ShapeDtypeStruct((LANES, r), F32)] * 3,
        scratch_shapes=[pltpu.VMEM((1, LANES), F32)],
        compiler_params=_cparams(("arbitrary", "arbitrary")), name="gate_logs")(h, w, bias, carry0)


def _cast_block(x_ref, o_ref):
    x = x_ref[...].astype(o_ref.dtype)
    pad = o_ref.shape[1] - x.shape[1]
    if pad:
        x = jnp.concatenate([x, jnp.zeros((x.shape[0], pad), o_ref.dtype)], axis=1)
    o_ref[...] = x


def cast_block_specs(w, layer, cols_out, grid):
    _, rows, cols = w.shape
    steps = grid[0] * grid[1]
    tr = rows // steps
    assert rows % steps == 0 and tr % 16 == 0 and cols % LANES == 0 and cols_out % LANES == 0
    return (pl.BlockSpec((None, tr, cols), lambda a, b: (layer, a * grid[1] + b, 0)),
            pl.BlockSpec((None, tr, cols_out), lambda a, b: (0, a * grid[1] + b, 0)),
            jax.ShapeDtypeStruct((1, rows, cols_out), BF16))


def _mlstm_head(q, k, v, bcol, brow, icol, irow, c0, n0, m0, nv):
    ln = q.shape[0]
    g = bcol + m0
    d = bcol - brow + irow
    ti = lax.broadcasted_iota(jnp.int32, (ln, ln), 0)
    si = lax.broadcasted_iota(jnp.int32, (ln, ln), 1)
    d = jnp.where(si <= ti, d, -jnp.inf)
    m = jnp.maximum(g, jnp.max(d, axis=1, keepdims=True))
    w = jnp.exp(d - m)
    inter = jnp.exp(g - m)
    s = lax.dot_general(q, k, NT_DIMS, preferred_element_type=F32) * w
    num = (jnp.dot(s.astype(BF16), v, preferred_element_type=F32)
           + inter * jnp.dot(q, c0.astype(BF16), preferred_element_type=F32))
    den = jnp.sum(s, axis=1, keepdims=True) + inter * jnp.sum(q.astype(F32) * n0, axis=1, keepdims=True)
    h = num / jnp.maximum(jnp.abs(den), jnp.exp(-m))
    m_end = m[nv - 1:nv]
    sidx = lax.broadcasted_iota(jnp.int32, (ln, 1), 0)
    w_end = jnp.where(sidx < nv, jnp.exp(bcol[nv - 1:nv] - bcol + icol - m_end), 0.0)
    decay = jnp.exp(g[nv - 1:nv] - m_end)
    kw = k.astype(F32) * w_end
    c1 = decay * c0 + jnp.dot(kw.T.astype(BF16), v, preferred_element_type=F32)
    n1 = decay * n0 + jnp.sum(kw, axis=0, keepdims=True)
    return h, c1, n1, m_end


def _mlstm_kernel(q_ref, k_ref, v_ref, og_ref, g_ref, l_ref, cb_ref, lt_ref, cbt_ref, ci_ref, ni_ref, mi_ref,
                  *refs, n_heads, dqk, dv, nv, with_cast):
    if with_cast:
        w_ref, hm_ref, co_ref, no_ref, mo_ref, wo_ref, c_s, n_s, m_s = refs
    else:
        hm_ref, co_ref, no_ref, mo_ref, c_s, n_s, m_s = refs
    c = pl.program_id(1)

    @pl.when(c == 0)
    def _():
        c_s[...] = ci_ref[...]
        n_s[...] = ni_ref[...]
        m_s[...] = mi_ref[...]

    logs, cb, logs_t, cb_t = l_ref[...], cb_ref[...], lt_ref[...], cbt_ref[...]
    for h in range(n_heads):
        hh, c1, n1, m_end = _mlstm_head(
            q_ref[:, h * dqk:(h + 1) * dqk], k_ref[:, h * dqk:(h + 1) * dqk], v_ref[:, h * dv:(h + 1) * dv],
            cb[:, n_heads + h:n_heads + h + 1], cb_t[n_heads + h:n_heads + h + 1, :],
            logs[:, h:h + 1], logs_t[h:h + 1, :], c_s[h], n_s[h], m_s[h][:, 0:1], nv)
        c_s[h] = c1
        n_s[h] = n1
        m_s[h] = jnp.broadcast_to(m_end, (1, LANES))
        cols = slice(h * dv, (h + 1) * dv)
        hn = hh * lax.rsqrt(jnp.mean(hh * hh, axis=-1, keepdims=True) + EPS) * g_ref[:, cols]
        hm_ref[:, cols] = (hn * jax.nn.sigmoid(og_ref[:, cols].astype(F32))).astype(hm_ref.dtype)
    if with_cast:
        _cast_block(w_ref, wo_ref)

    @pl.when(c == pl.num_programs(1) - 1)
    def _():
        co_ref[0] = c_s[...]
        no_ref[0] = n_s[...]
        mo_ref[0] = m_s[...]


def mlstm_chunks(qkv, og, mh_g, logs, cb, logs_t, cb_t, c_init, n_init, m_init, n_seq, nv, cast=None):
    r = qkv.shape[0]
    n_heads, dqk, dv = c_init.shape
    nc = r // n_seq // CHUNK
    hq, hv = n_heads * dqk, n_heads * dv
    assert hv == 2 * hq
    rows = lambda b, c: b * nc + c
    kern = functools.partial(_mlstm_kernel, n_heads=n_heads, dqk=dqk, dv=dv, nv=nv, with_cast=cast is not None)
    full3 = lambda shape: pl.BlockSpec(shape, lambda b, c: (0, 0, 0))
    cast_in, cast_out, cast_shape = cast_block_specs(*cast, (n_seq, nc)) if cast else (None, None, None)
    return pl.pallas_call(
        kern, grid=(n_seq, nc),
        in_specs=[pl.BlockSpec((CHUNK, hq), lambda b, c: (rows(b, c), 0)),
                  pl.BlockSpec((CHUNK, hq), lambda b, c: (rows(b, c), 1)),
                  pl.BlockSpec((CHUNK, hv), lambda b, c: (rows(b, c), 1)),
                  pl.BlockSpec((CHUNK, hv), lambda b, c: (rows(b, c), 0)),
                  pl.BlockSpec((1, hv), lambda b, c: (0, 0)),
                  pl.BlockSpec((CHUNK, LANES), lambda b, c: (rows(b, c), 0)),
                  pl.BlockSpec((CHUNK, LANES), lambda b, c: (rows(b, c), 0)),
                  pl.BlockSpec((LANES, CHUNK), lambda b, c: (0, rows(b, c))),
                  pl.BlockSpec((LANES, CHUNK), lambda b, c: (0, rows(b, c))),
                  full3(c_init.shape), full3(n_init.shape), full3(m_init.shape)] + ([cast_in] if cast else []),
        out_specs=[pl.BlockSpec((CHUNK, hv), lambda b, c: (rows(b, c), 0)),
                   pl.BlockSpec((1,) + c_init.shape, lambda b, c: (b, 0, 0, 0)),
                   pl.BlockSpec((1,) + n_init.shape, lambda b, c: (b, 0, 0, 0)),
                   pl.BlockSpec((1,) + m_init.shape, lambda b, c: (b, 0, 0, 0))] + ([cast_out] if cast else []),
        out_shape=[jax.ShapeDtypeStruct((r, hv), BF16),
                   jax.ShapeDtypeStruct((n_seq,) + c_init.shape, F32),
                   jax.ShapeDtypeStruct((n_seq,) + n_init.shape, F32),
                   jax.ShapeDtypeStruct((n_seq,) + m_init.shape, F32)] + ([cast_shape] if cast else []),
        scratch_shapes=[pltpu.VMEM(c_init.shape, F32), pltpu.VMEM(n_init.shape, F32), pltpu.VMEM(m_init.shape, F32)],
        compiler_params=_cparams(("arbitrary", "arbitrary")), name="mlstm_chunks")(
            qkv, qkv, qkv, og, mh_g, logs, cb, logs_t, cb_t, c_init, n_init, m_init, *([cast[0]] if cast else []))


def _mlstm_step_kernel(q_ref, k_ref, v_ref, og_ref, g_ref, li_ref, lf_ref, c_ref, n_ref, m_ref,
                       hm_ref, co_ref, no_ref, mo_ref, *, n_heads):
    q, k = q_ref[0].astype(F32), k_ref[0].astype(F32)
    v = v_ref[0].astype(F32)
    og = og_ref[0].astype(F32)
    pad = jnp.zeros((LANES - n_heads, q.shape[1]), F32)
    q_t = jnp.concatenate([q, pad], axis=0).T
    k_t = jnp.concatenate([k, pad], axis=0).T
    logi, logf, m0 = li_ref[0], lf_ref[0], m_ref[0]
    g = logf + m0
    m = jnp.maximum(g, logi)
    w = jnp.exp(logi - m)
    inter = jnp.exp(g - m)
    floor = jnp.exp(-m)
    for h in range(n_heads):
        c0, n0 = c_ref[0, h], n_ref[0, h]
        qh, kh, vh = q[h:h + 1], k[h:h + 1], v[h:h + 1]
        wh, ih = w[h:h + 1], inter[h:h + 1]
        s = jnp.sum(qh * kh, axis=1, keepdims=True) * wh
        num = s * vh + ih * jnp.sum(q_t[:, h:h + 1] * c0, axis=0, keepdims=True)
        den = s + ih * jnp.sum(qh * n0, axis=1, keepdims=True)
        hh = num / jnp.maximum(jnp.abs(den), floor[h:h + 1])
        co_ref[0, h] = ih * c0 + (wh * k_t[:, h:h + 1]) * vh
        no_ref[0, h] = ih * n0 + wh * kh
        hn = hh * lax.rsqrt(jnp.mean(hh * hh, axis=-1, keepdims=True) + EPS) * g_ref[h:h + 1]
        hm_ref[0, h:h + 1] = hn * jax.nn.sigmoid(og[h:h + 1])
    mo_ref[0] = m


def mlstm_step(q, k, v, og, mh_g, logi, logf, c0, n0, m0):
    nb, n_heads, dqk = q.shape
    dv = v.shape[2]
    b3 = lambda a: pl.BlockSpec((1,) + a.shape[1:], lambda i: (i,) + (0,) * (a.ndim - 1))
    outs = [jax.ShapeDtypeStruct((nb, n_heads, dv), F32), jax.ShapeDtypeStruct(c0.shape, F32),
            jax.ShapeDtypeStruct(n0.shape, F32), jax.ShapeDtypeStruct(m0.shape, F32)]
    return pl.pallas_call(
        functools.partial(_mlstm_step_kernel, n_heads=n_heads), grid=(nb,),
        in_specs=[b3(q), b3(k), b3(v), b3(og), pl.BlockSpec(mh_g.shape, lambda i: (0, 0)),
                  b3(logi), b3(logf), b3(c0), b3(n0), b3(m0)],
        out_specs=[b3(o) for o in outs], out_shape=outs,
        compiler_params=_cparams(("parallel",)), name="mlstm_step")(q, k, v, og, mh_g, logi, logf, c0, n0, m0)


def _fox_kernel(q_ref, k_ref, v_ref, km_ref, vm_ref, cf_ref, cft_ref, cmt_ref, *refs,
                tq, tk, n_meta, scale, lane0, causal_meta, with_cast):
    if with_cast:
        w_ref, o_ref, wo_ref, kb, vb = refs
        _cast_block(w_ref, wo_ref)
    else:
        o_ref, kb, vb = refs
    h = pl.program_id(1)
    seq = q_ref.shape[0]
    if not causal_meta:
        kb[...] = k_ref[...].astype(BF16)
        vb[...] = v_ref[...].astype(BF16)
    km = km_ref[...].astype(BF16)
    vm = vm_ref[...].astype(BF16)
    lane = lax.broadcasted_iota(jnp.int32, (tq, LANES), 1)
    row_id = lax.broadcasted_iota(jnp.int32, (tq, tk), 0)
    col_id = lax.broadcasted_iota(jnp.int32, (tq, tk), 1)

    scale2 = scale * LOG2E

    def online(carry, s, vals):
        m, l, acc = carry
        mn = jnp.maximum(m, jnp.max(s, axis=1, keepdims=True))
        alpha = jnp.exp2(m - mn)
        p = jnp.exp2(s - mn)
        return (mn, alpha * l + jnp.sum(p, axis=1, keepdims=True),
                alpha * acc + jnp.dot(p.astype(BF16), vals, preferred_element_type=F32))

    for i in range(seq // tq):
        rows = slice(i * tq, (i + 1) * tq)
        q = q_ref[rows, :]
        fq = jnp.sum(jnp.where(lane == lane0 + h, cf_ref[rows, :], 0.0), axis=1, keepdims=True) * LOG2E
        s = lax.dot_general(q, km, NT_DIMS, preferred_element_type=F32) * scale2 + fq - cmt_ref[...] * LOG2E
        col = lax.broadcasted_iota(jnp.int32, s.shape, 1)
        ok = col < n_meta
        if causal_meta:
            ok = ok & (col <= lax.broadcasted_iota(jnp.int32, s.shape, 0))
        s = jnp.where(ok, s, -jnp.inf)
        m = jnp.max(s, axis=1, keepdims=True)
        p = jnp.exp2(s - m)
        carry = (m, jnp.sum(p, axis=1, keepdims=True), jnp.dot(p.astype(BF16), vm, preferred_element_type=F32))
        if not causal_meta:
            for j in range(-(-(i + 1) * tq // tk)):
                cols = slice(j * tk, (j + 1) * tk)
                s = (lax.dot_general(q, kb[cols, :], NT_DIMS, preferred_element_type=F32) * scale2
                     + fq - cft_ref[j:j + 1, :] * LOG2E)
                if (j + 1) * tk - 1 > i * tq:
                    s = jnp.where(col_id + (j * tk - i * tq) <= row_id, s, -jnp.inf)
                carry = online(carry, s, vb[cols, :])
        o_ref[rows, :] = (carry[2] / carry[1]).astype(o_ref.dtype)


def fox_prompt(q, k, v, k_meta, v_meta, cf, cft, cmt, n_seq, n_meta, lane0, scale, causal_meta, cast=None):
    r, width = q.shape
    dh = LANES
    n_heads = width // dh
    seq = r // n_seq
    tq, tk = min(FOX_TQ, seq), min(FOX_TK, seq)
    s_rows = k_meta.shape[0]
    cft3 = cft.reshape(LANES, n_seq, seq // tk, tk)
    cmt3 = cmt.reshape(LANES, 1, s_rows)
    kern = functools.partial(_fox_kernel, tq=tq, tk=tk, n_meta=n_meta, scale=scale, lane0=lane0,
                             causal_meta=causal_meta, with_cast=cast is not None)
    blk = pl.BlockSpec((seq, dh), lambda b, h: (b, h))
    cast_in, cast_out, cast_shape = cast_block_specs(*cast, (n_seq, n_heads)) if cast else (None, None, None)
    out_shape = jax.ShapeDtypeStruct((r, width), BF16)
    return pl.pallas_call(
        kern, grid=(n_seq, n_heads),
        in_specs=[blk, blk, blk,
                  pl.BlockSpec((s_rows, dh), lambda b, h: (0, h)),
                  pl.BlockSpec((s_rows, dh), lambda b, h: (0, h)),
                  pl.BlockSpec((seq, LANES), lambda b, h: (b, 0)),
                  pl.BlockSpec((None, None, seq // tk, tk), lambda b, h: (lane0 + h, b, 0, 0)),
                  pl.BlockSpec((None, 1, s_rows), lambda b, h: (lane0 + h, 0, 0))] + ([cast_in] if cast else []),
        out_specs=[blk, cast_out] if cast else blk,
        out_shape=[out_shape, cast_shape] if cast else out_shape,
        scratch_shapes=[pltpu.VMEM((seq, dh), BF16), pltpu.VMEM((seq, dh), BF16)],
        compiler_params=_cparams(("parallel", "arbitrary")), name="fox_prompt")(
            q, k, v, k_meta, v_meta, cf, cft3, cmt3, *([cast[0]] if cast else []))


def _page_logs_kernel(x_ref, r_ref, t_ref, *, n_heads, rpp, sub_pages):
    hi = lax.Precision.HIGHEST
    li = lax.broadcasted_iota(jnp.int32, (LANES, LANES), 0)
    lo = lax.broadcasted_iota(jnp.int32, (LANES, LANES), 1)
    same_head = (li % n_heads) == (lo % n_heads)
    w_row = (same_head & (li // n_heads >= lo // n_heads)).astype(F32)
    w_all = same_head.astype(F32)
    sb = sub_pages * rpp
    ri = lax.broadcasted_iota(jnp.int32, (sb, sb), 0)
    ro = lax.broadcasted_iota(jnp.int32, (sb, sb), 1)
    later_rows = ((ro > ri) & (ro // rpp == ri // rpp)).astype(F32)
    pi = lax.broadcasted_iota(jnp.int32, (sub_pages * 8, sb), 0)
    po = lax.broadcasted_iota(jnp.int32, (sub_pages * 8, sb), 1)
    page_rows = (po // rpp == pi // 8).astype(F32)
    for s in range(x_ref.shape[0] // sb):
        x = x_ref[s * sb:(s + 1) * sb, :]
        row_tot = jnp.dot(x, w_all, precision=hi, preferred_element_type=F32)
        r_ref[s * sb:(s + 1) * sb, :] = (jnp.dot(x, w_row, precision=hi, preferred_element_type=F32)
                                         + jnp.dot(later_rows, row_tot, precision=hi, preferred_element_type=F32) - x)
        t_ref[s * sub_pages * 8:(s + 1) * sub_pages * 8, :] = jnp.dot(page_rows, row_tot, precision=hi,
                                                                      preferred_element_type=F32)


def page_logs(cache_lf):
    depth, n_pool, page, n_heads = cache_lf.shape
    assert LANES % n_heads == 0 and (page * n_heads) % (8 * LANES) == 0
    rpp = page * n_heads // LANES
    n_pages = depth * n_pool
    sub_pages = _largest_divisor(n_pages, max(1, LANES // rpp))
    tile_pages = sub_pages * _largest_divisor(n_pages // sub_pages, 8)
    x = cache_lf.reshape(n_pages * rpp, LANES)
    kern = functools.partial(_page_logs_kernel, n_heads=n_heads, rpp=rpp, sub_pages=sub_pages)
    r, t = pl.pallas_call(
        kern, grid=(n_pages // tile_pages,),
        in_specs=[pl.BlockSpec((tile_pages * rpp, LANES), lambda i: (i, 0))],
        out_specs=[pl.BlockSpec((tile_pages * rpp, LANES), lambda i: (i, 0)),
                   pl.BlockSpec((tile_pages * 8, LANES), lambda i: (i, 0))],
        out_shape=[jax.ShapeDtypeStruct((n_pages * rpp, LANES), F32), jax.ShapeDtypeStruct((n_pages * 8, LANES), F32)],
        compiler_params=_cparams(("parallel",)), name="page_logs")(x)
    return r.reshape(depth, n_pool, rpp, LANES), t.reshape(depth, n_pool, 8, LANES)


def _decode_refs(refs, pages_per_step):
    q_ref, kn_ref, vn_ref, fq_ref = refs[1:5]
    page_refs = refs[5:5 + 4 * pages_per_step]
    o_ref, m_s, l_s, acc, carry = refs[5 + 4 * pages_per_step:]
    return q_ref, kn_ref, vn_ref, fq_ref, page_refs, o_ref, m_s, l_s, acc, carry


def _fox_decode_init(*refs, scale, pages_per_step):
    q_ref, kn_ref, vn_ref, _, _, _, m_s, l_s, acc, carry = _decode_refs(refs, pages_per_step)

    @pl.when(pl.program_id(1) == 0)
    def _():
        m_s[...] = jnp.sum(q_ref[...].astype(F32) * kn_ref[...], axis=1, keepdims=True) * scale
        l_s[...] = jnp.ones_like(l_s)
        acc[...] = vn_ref[...]
        carry[...] = jnp.zeros_like(carry)


def _fox_decode_pages(*refs, n_heads, scale, pages_per_step):
    q_ref, _, _, fq_ref, page_refs, _, m_s, l_s, acc, carry = _decode_refs(refs, pages_per_step)
    q = q_ref[...]
    fq = fq_ref[...]
    later_pages = carry[0:1, :]
    m, l, a = m_s[...], l_s[...], acc[...]
    for u in range(pages_per_step):
        k_ref, v_ref, r_ref, t_ref = page_refs[4 * u:4 * u + 4]
        later = r_ref[...] + later_pages
        later_pages = later_pages + t_ref[0:1, :]
        bias = jnp.concatenate([jnp.broadcast_to(later[c:c + 1], (n_heads, LANES)) for c in range(later.shape[0])],
                               axis=1)
        s = lax.dot_general(q, k_ref[...].astype(BF16), NT_DIMS, preferred_element_type=F32) * scale + fq + bias
        own = (lax.broadcasted_iota(jnp.int32, s.shape, 1) % n_heads) == lax.broadcasted_iota(jnp.int32, s.shape, 0)
        s = jnp.where(own, s, -jnp.inf)
        mn = jnp.maximum(m, jnp.max(s, axis=1, keepdims=True))
        alpha = jnp.exp(m - mn)
        pr = jnp.exp(s - mn)
        l = alpha * l + jnp.sum(pr, axis=1, keepdims=True)
        a = alpha * a + jnp.dot(pr.astype(BF16), v_ref[...].astype(BF16), preferred_element_type=F32)
        m = mn
    m_s[...], l_s[...], acc[...] = m, l, a
    carry[...] = jnp.broadcast_to(later_pages, carry.shape)


def _fox_decode_finish(*refs, pages_per_step):
    _, _, _, _, _, o_ref, _, l_s, acc, _ = _decode_refs(refs, pages_per_step)

    @pl.when(pl.program_id(1) == pl.num_programs(1) - 1)
    def _():
        o_ref[...] = acc[...] / l_s[...]


def fox_decode_job(page_table, layer, q, k_new, v_new, fq, cache_k, cache_v, later_logs, page_totals, scale,
                   steps_per_seq=None):
    nb, n_pages = page_table.shape
    depth, n_pool, page, n_heads, dh = cache_k.shape
    rows = page * n_heads
    ck = cache_k.reshape(depth, n_pool, rows, dh)
    cv = cache_v.reshape(depth, n_pool, rows, dh)
    pps = n_pages // steps_per_seq if steps_per_seq else _largest_divisor(n_pages, 8)
    assert n_pages % pps == 0
    phases = (functools.partial(_fox_decode_init, scale=scale, pages_per_step=pps),
              functools.partial(_fox_decode_pages, n_heads=n_heads, scale=scale, pages_per_step=pps),
              functools.partial(_fox_decode_finish, pages_per_step=pps))
    per_b = lambda w: pl.BlockSpec((None, n_heads, w), lambda i, p, pt: (i, 0, 0))
    paged = lambda a, u: pl.BlockSpec((None, None) + a.shape[2:],
                                      lambda i, p, pt: (layer, pt[i, n_pages - 1 - (p * pps + u)], 0, 0))
    page_specs, page_args = [], []
    for u in range(pps):
        for a in (ck, cv, later_logs, page_totals):
            page_specs.append(paged(a, u))
            page_args.append(a)
    return Rider(
        grid=(nb, n_pages // pps), prefetch=[page_table],
        in_specs=[per_b(dh), per_b(dh), per_b(dh), per_b(1)] + page_specs, args=[q, k_new, v_new, fq] + page_args,
        out_specs=[per_b(dh)], out_shape=[jax.ShapeDtypeStruct((nb, n_heads, dh), F32)],
        scratch_shapes=[pltpu.VMEM((n_heads, 1), F32), pltpu.VMEM((n_heads, 1), F32), pltpu.VMEM((n_heads, dh), F32),
                        pltpu.VMEM((8, LANES), F32)],
        phases=phases)


def run_alone(job, name):
    def body(*refs):
        for phase in job.phases:
            phase(*refs)

    grid_spec = pltpu.PrefetchScalarGridSpec(
        num_scalar_prefetch=len(job.prefetch), grid=job.grid, in_specs=job.in_specs, out_specs=job.out_specs,
        scratch_shapes=job.scratch_shapes)
    return pl.pallas_call(body, grid_spec=grid_spec, out_shape=job.out_shape,
                          compiler_params=_cparams(("arbitrary",) * len(job.grid)), name=name)(
                              *job.prefetch, *job.args)


def kernel(x_prompt, x_sample, cache_k, cache_v, cache_lf, state_c, state_n, state_m, page_table, meta, norm1_g,
           w_in, b_in, q_norm_g, k_norm_g, mh_norm_g, p_a, p_b, w_out, norm2_g, w_gate, w_up, w_down):
    nb, seq, d = x_prompt.shape
    db = x_sample.shape[0]
    assert x_sample.shape[1] == 1
    n_meta = meta.shape[0]
    depth = w_in.shape[0]
    _, _, hm, dqk, dv = state_c.shape
    _, _, _, hf, dh = cache_k.shape
    d_ff = w_gate.shape[2]
    assert dh == LANES and dqk == LANES and seq % CHUNK == 0 and n_meta + db <= SMALL_ROWS
    assert 2 * hm + hf <= LANES
    r = nb * seq
    sizes = (hm * dqk, hm * dqk, hm * dv, hm * dv, hm, hm, hf * dh, hf * dh, hf * dh, hf, d, d)
    st = [0]
    for sz in sizes:
        st.append(st[-1] + sz)
    n_gate = 2 * hm + hf
    n_mqkv, n_mo, n_f = st[3], hm * dv, hf * dh
    d_ffp = d_ff + (-d_ff % 1024)
    scale_f = dh ** -0.5
    sample = slice(n_meta, n_meta + db)

    w_in_t = jnp.swapaxes(w_in, 1, 2)
    w_m = cast_weights_t(w_in_t, 0, st[4], "cast_w_m")
    w_f = cast_weights_t(w_in_t, st[6], 3 * n_f, "cast_w_f")
    w_g = cast_weights_t(w_in_t, st[10], 2 * d, "cast_w_g")
    w_small = gate_weights_t(w_in_t, st[4], 2 * hm, st[9], hf)

    xb = x_prompt.reshape(r, d)
    xs = jnp.concatenate([meta.astype(F32), x_sample.reshape(db, d),
                          jnp.zeros((SMALL_ROWS - n_meta - db, d), F32)], axis=0)
    qk_scale = jnp.concatenate([jnp.ones((1, hm * dqk), F32), jnp.full((1, hm * dqk), dqk ** -0.5, F32),
                                jnp.ones((1, hm * dv), F32)], axis=1)
    zero_row = jnp.zeros((1, LANES), F32)
    later_logs, page_totals = page_logs(cache_lf)
    outs = [[] for _ in range(12)]

    for l in range(depth):
        b_l = b_in[l].reshape(1, -1)
        b_small = jnp.concatenate([b_l[:, st[4]:st[6]], b_l[:, st[9]:st[10]], jnp.zeros((1, LANES - n_gate), F32)], axis=1)

        hb, hs = rmsnorm_rows(xb, norm1_g[l]), rmsnorm_rows(xs, norm1_g[l])
        lhs = [(hb, hs)]
        tm_in = min(1024, r)
        qkv, qkv_s, wo = fused_matmul(lhs, l, [(w_m, 0, 0, b_l[:, :n_mqkv])], [], [qk_scale], _epi_scale, n_mqkv, BF16,
                                      tm_in, 1024, "in_qkv",
                                      rider=cast_rider(w_out, l, (n_mqkv // 1024, r // tm_in), d))
        og, og_s, pa = fused_matmul(lhs, l, [(w_m, 0, n_mqkv, b_l[:, st[3]:st[4]])], [], [], _epi_identity, n_mo, BF16,
                                    tm_in, 1024, "in_mo", rider=cast_rider(p_a, l, (n_mo // 1024, r // tm_in), n_mo))
        fq, fq_s, pb = fused_matmul(lhs, l, [(w_f, 0, 0, b_l[:, st[6]:st[7]])], [], [q_norm_g[l].reshape(1, dh)],
                                    _epi_headnorm, n_f, BF16, tm_in, 1024, "in_fq",
                                    rider=cast_rider(p_b, l, (n_f // 1024, r // tm_in), n_f))
        fk, fk_s = fused_matmul(lhs, l, [(w_f, 0, n_f, b_l[:, st[7]:st[8]])], [], [k_norm_g[l].reshape(1, dh)],
                                _epi_headnorm, n_f, F32, 1024, 1024, "in_fk")
        fv, fv_s = fused_matmul(lhs, l, [(w_f, 0, 2 * n_f, b_l[:, st[8]:st[9]])], [], [], _epi_identity, n_f, F32,
                                1024, 1024, "in_fv")
        ls, cbs, _, lst, cbst, _ = gate_logs(hs, w_small, l, b_small, zero_row, 1, hm, n_gate)
        decode_in = (page_table, l, fq_s[sample].reshape(db, hf, dh), fk_s[sample].reshape(db, hf, dh),
                     fv_s[sample].reshape(db, hf, dh), ls[sample, 2 * hm:n_gate].reshape(db, hf, 1),
                     cache_k, cache_v, later_logs, page_totals, scale_f)
        page_steps = page_table.shape[1] // 4 if page_table.shape[1] % 4 == 0 else 0
        ride = 2 * d // 1024 == db and page_steps and r % page_steps == 0 and (r // page_steps) % LANES == 0
        if ride:
            gab, gab_s, hf_d = fused_matmul(lhs, l, [(w_g, 0, 0, b_l[:, st[10]:])], [], [], _epi_identity, 2 * d, BF16,
                                            r // page_steps, 1024, "in_gab_decode",
                                            rider=fox_decode_job(*decode_in, steps_per_seq=page_steps))
        else:
            gab, gab_s = fused_matmul(lhs, l, [(w_g, 0, 0, b_l[:, st[10]:])], [], [], _epi_identity, 2 * d, BF16,
                                      1024, 1024, "in_gab")
            hf_d, = run_alone(fox_decode_job(*decode_in), "fox_decode")
        lb, cbb, cfb, lbt, cbbt, cfbt = gate_logs(hb, w_small, l, b_small, cbs[n_meta - 1:n_meta], nb, hm, n_gate)

        mh_g = mh_norm_g[l].reshape(1, hm * dv)
        hm_meta, c_m, n_m, m_m = mlstm_chunks(
            qkv_s, og_s, mh_g, ls, cbs, lst, cbst, jnp.zeros((hm, dqk, dv), F32), jnp.zeros((hm, 1, dqk), F32),
            jnp.zeros((hm, 1, LANES), F32), 1, n_meta)
        hm_b, c_p, n_p, m_p, wg = mlstm_chunks(qkv, og, mh_g, lb, cbb, lbt, cbbt, c_m[0], n_m[0], m_m[0], nb, CHUNK,
                                               cast=(w_gate, l, d_ffp))
        hm_d, c_d, n_d, m_d = mlstm_step(
            qkv_s[sample, :hm * dqk].reshape(db, hm, dqk), qkv_s[sample, hm * dqk:2 * hm * dqk].reshape(db, hm, dqk),
            qkv_s[sample, 2 * hm * dqk:].reshape(db, hm, dv), og_s[sample].reshape(db, hm, dv),
            mh_g.reshape(hm, dv), ls[sample, :hm].reshape(db, hm, 1), ls[sample, hm:2 * hm].reshape(db, hm, 1),
            state_c[l], state_n[l].reshape(db, hm, 1, dqk), state_m[l].reshape(db, hm, 1))
        pad_rows = jnp.zeros((SMALL_ROWS - n_meta - db, hm * dv), BF16)
        hm_s = jnp.concatenate([hm_meta[:n_meta], hm_d.reshape(db, hm * dv).astype(BF16), pad_rows], axis=0)

        hf_meta = fox_prompt(fq_s, fk_s, fv_s, fk_s, fv_s, cbs, cbst, cbst, 1, n_meta, 2 * hm, scale_f, True)
        hf_b, wu = fox_prompt(fq, fk, fv, fk_s, fv_s, cfb, cfbt, cbst, nb, n_meta, 2 * hm, scale_f, False,
                              cast=(w_up, l, d_ffp))
        hf_s = jnp.concatenate([hf_meta[:n_meta], hf_d.reshape(db, hf * dh).astype(BF16), pad_rows], axis=0)

        tm_mg = min(512, r)
        u, u_s, wd = fused_matmul([(hm_b, hm_s), (hf_b, hf_s)], 0, [(pa, 0, 0, None), (pb, 1, 0, None)],
                                  [(gab, gab_s, 0), (gab, gab_s, d)], [], _epi_gated2, d, BF16, tm_mg, 1024,
                                  "merge_gate", rider=cast_rider(w_down, l, (d // 1024, r // tm_mg), d_ffp))
        x1, x1_s = fused_matmul([(u, u_s)], 0, [(wo, 0, 0, None)], [(xb, xs, 0)], [], _epi_residual, d, F32,
                                512, 1024, "out_proj")
        h2, h2_s = rmsnorm_rows(x1, norm2_g[l]), rmsnorm_rows(x1_s, norm2_g[l])
        act, act_s = fused_matmul([(h2, h2_s)], 0, [(wg, 0, 0, None), (wu, 0, 0, None)], [], [], _epi_swiglu,
                                  d_ffp, BF16, 512, 1024, "ffn_up")
        xb, xs = fused_matmul([(act, act_s)], 0, [(wd, 0, 0, None)], [(x1, x1_s, 0)], [], _epi_residual, d, F32,
                              256, 1024, "ffn_down", single_buffer_w=True)

        outs[0].append((fk_s.reshape(SMALL_ROWS, hf, dh), fk.reshape(r, hf, dh)))
        outs[1].append((fv_s.reshape(SMALL_ROWS, hf, dh), fv.reshape(r, hf, dh)))
        outs[2].append((ls[:, 2 * hm:n_gate], lb[:, 2 * hm:n_gate]))
        outs[3].append(c_p)
        outs[4].append(n_p.reshape(nb, hm, dqk))
        outs[5].append(m_p[:, :, 0, 0])
        outs[6].append(fk_s[sample].reshape(db, 1, hf, dh))
        outs[7].append(fv_s[sample].reshape(db, 1, hf, dh))
        outs[8].append(ls[sample, 2 * hm:n_gate].reshape(db, 1, hf))
        outs[9].append(c_d)
        outs[10].append(n_d.reshape(db, hm, dqk))
        outs[11].append(m_d.reshape(db, hm))

    def with_meta(pairs):
        pieces = [p for small, big in pairs for b in range(nb) for p in (small[:n_meta], big[b * seq:(b + 1) * seq])]
        return jnp.concatenate(pieces, axis=0).reshape((depth, nb, n_meta + seq) + pieces[0].shape[1:])

    return ((xb.reshape(nb, seq, d), xs[sample].reshape(db, 1, d)) + tuple(with_meta(o) for o in outs[:3])
            + tuple(jnp.stack(o) for o in outs[3:]))
```
